```python
import math
import jax
import jax.numpy as jnp
from jax import lax
import numpy as np

D_MODEL = 2048
BATCH = 16
SEQ = 256
DEPTH = 4
DEC_BATCH = 8
DEC_SEQ = 2048
PAST_LEN = 512

GRID_W = 64
ROPE_THETA = 10000.0
EPS = 1e-6
Q_BLOCK = 128
N_MOD = 9
D_FF = 2 * D_MODEL

MLA_HEADS = 8
MLA_Q_RANK = D_MODEL // 4
MLA_KV_RANK = D_MODEL // 8
MLA_NOPE = 128
MLA_ROPE = 64
MLA_V = 128

S5_CH = D_MODEL // 4
S5_GROUP = 16
S5_GROUPS = S5_CH // S5_GROUP
S5_STATE = 64

GQA_Q_HEADS = 4
GQA_KV_HEADS = 2
GQA_HEAD_DIM = 128

MLA_IN = MLA_Q_RANK + MLA_KV_RANK + MLA_ROPE
GQA_IN = (GQA_Q_HEADS + 2 * GQA_KV_HEADS) * GQA_HEAD_DIM
D_IN = MLA_IN + S5_CH + GQA_IN
D_MIX = MLA_HEADS * MLA_V + S5_CH + GQA_Q_HEADS * GQA_HEAD_DIM
SPLIT_POINTS = (MLA_Q_RANK, MLA_Q_RANK + MLA_KV_RANK, MLA_IN, MLA_IN + S5_CH,
                MLA_IN + S5_CH + GQA_Q_HEADS * GQA_HEAD_DIM,
                MLA_IN + S5_CH + (GQA_Q_HEADS + GQA_KV_HEADS) * GQA_HEAD_DIM)
MLA_SCALE = 1.0 / math.sqrt(MLA_NOPE + MLA_ROPE)
GQA_SCALE = 1.0 / math.sqrt(GQA_HEAD_DIM)

kernel_name = "hymba_mla_s5_gqa_prefix_dit_step"


def rms_norm(x, g):
    xf = x.astype(jnp.float32)
    y = xf * lax.rsqrt(jnp.mean(xf * xf, axis=-1, keepdims=True) + EPS)
    return (y * g.astype(jnp.float32)).astype(x.dtype)


def modulate(x, g, shift, scale):
    return rms_norm(x, g) * (1.0 + scale) + shift


def swiglu(h, w_gate, w_up, w_down):
    return (jax.nn.silu(h @ w_gate) * (h @ w_up)) @ w_down


def modulation(cvec, w_ada, b_ada):
    m = jax.nn.silu(cvec) @ w_ada + b_ada
    return jnp.split(m, N_MOD, axis=-1)


def grid_angles(seq_len, rot_dim):
    n_rows = seq_len // GRID_W
    row = jnp.broadcast_to(jnp.arange(n_rows, dtype=jnp.float32)[:, None], (n_rows, GRID_W)).reshape(seq_len)
    col = jnp.broadcast_to(jnp.arange(GRID_W, dtype=jnp.float32)[None, :], (n_rows, GRID_W)).reshape(seq_len)
    n_freq = rot_dim // 4
    inv = ROPE_THETA ** (-jnp.arange(n_freq, dtype=jnp.float32) / n_freq)
    ang = jnp.concatenate([row[:, None] * inv, col[:, None] * inv], axis=-1)
    return jnp.cos(ang), jnp.sin(ang)


def apply_rope(x, cos, sin):
    half = x.shape[-1] // 2
    x1, x2 = x[..., :half], x[..., half:]
    c = cos[None, :, None, :].astype(x.dtype)
    s = sin[None, :, None, :].astype(x.dtype)
    return jnp.concatenate([x1 * c - x2 * s, x2 * c + x1 * s], axis=-1)


def attention(q, k, v, scale):
    b, sq, hq, dk = q.shape
    hkv = k.shape[2]
    rep = hq // hkv
    nblk = sq // Q_BLOCK
    qb = q.reshape(b, nblk, Q_BLOCK, hkv, rep, dk).transpose(1, 0, 2, 3, 4, 5)

    def one_block(qi):
        s = jnp.einsum('bqgrd,bkgd->bgrqk', qi, k, preferred_element_type=jnp.float32) * scale
        p = jax.nn.softmax(s, axis=-1)
        return jnp.einsum('bgrqk,bkgd->bqgrd', p.astype(v.dtype), v)

    o = lax.map(one_block, qb)
    return o.transpose(1, 0, 2, 3, 4, 5).reshape(b, sq, hq, v.shape[-1])


def mla_queries(zq, q_norm, w_uq, cos, sin):
    b, s, _ = zq.shape
    q = (rms_norm(zq, q_norm) @ w_uq).reshape(b, s, MLA_HEADS, MLA_NOPE + MLA_ROPE)
    q_nope, q_rope = q[..., :MLA_NOPE], q[..., MLA_NOPE:]
    if cos is not None:
        q_rope = apply_rope(q_rope, cos, sin)
    return jnp.concatenate([q_nope, q_rope], axis=-1)


def mla_keys_values(ckv, k_rope, w_ukv, cos, sin):
    b, s, _ = ckv.shape
    kv = (ckv @ w_ukv).reshape(b, s, MLA_HEADS, MLA_NOPE + MLA_V)
    k_nope, v = kv[..., :MLA_NOPE], kv[..., MLA_NOPE:]
    kr = k_rope[:, :, None, :]
    if cos is not None:
        kr = apply_rope(kr, cos, sin)
    k = jnp.concatenate([k_nope, jnp.broadcast_to(kr, (b, s, MLA_HEADS, MLA_ROPE))], axis=-1)
    return k, v


def s5_discretize(lam_re, lam_im, log_dt, b_re, b_im):
    dt = jnp.exp(log_dt.astype(jnp.float32))[:, None]
    lr = lam_re.astype(jnp.float32)
    li = lam_im.astype(jnp.float32)
    mag = jnp.exp(lr * dt)
    ab_re = mag * jnp.cos(li * dt)
    ab_im = mag * jnp.sin(li * dt)
    den = lr * lr + li * li
    nr = ab_re - 1.0
    f_re = (nr * lr + ab_im * li) / den
    f_im = (ab_im * lr - nr * li) / den
    br = b_re.astype(jnp.float32)
    bi = b_im.astype(jnp.float32)
    bb_re = f_re[..., None] * br - f_im[..., None] * bi
    bb_im = f_re[..., None] * bi + f_im[..., None] * br
    return ab_re, ab_im, bb_re, bb_im


def s5_combine(e1, e2):
    a1r, a1i, b1r, b1i = e1
    a2r, a2i, b2r, b2i = e2
    return (a1r * a2r - a1i * a2i, a1r * a2i + a1i * a2r,
            a2r * b1r - a2i * b1i + b2r, a2r * b1i + a2i * b1r + b2i)


def s5_scan(uf, ab_re, ab_im, bb_re, bb_im, h0, reverse):
    bu_re = jnp.einsum('gpc,bsgc->bsgp', bb_re, uf)
    bu_im = jnp.einsum('gpc,bsgc->bsgp', bb_im, uf)
    if h0 is not None:
        idx = -1 if reverse else 0
        bu_re = bu_re.at[:, idx].add(ab_re * h0[..., 0] - ab_im * h0[..., 1])
        bu_im = bu_im.at[:, idx].add(ab_re * h0[..., 1] + ab_im * h0[..., 0])
    a_re = jnp.broadcast_to(ab_re, bu_re.shape)
    a_im = jnp.broadcast_to(ab_im, bu_re.shape)
    _, _, hr, hi = lax.associative_scan(s5_combine, (a_re, a_im, bu_re, bu_im), axis=1, reverse=reverse)
    return hr, hi


def s5_mixer(u, lam_re, lam_im, log_dt, b_re, b_im, c_re, c_im, d_skip, w_glu, b_glu, h0, return_state):
    bsz, s, _ = u.shape
    uf = u.astype(jnp.float32).reshape(bsz, s, S5_GROUPS, S5_GROUP)
    y = d_skip.astype(jnp.float32).reshape(S5_GROUPS, S5_GROUP) * uf
    finals = []
    for dirn, reverse in ((0, False), (1, True)):
        ab_re, ab_im, bb_re, bb_im = s5_discretize(lam_re[dirn], lam_im[dirn], log_dt[dirn], b_re[dirn], b_im[dirn])
        init = None if h0 is None else h0[:, dirn].astype(jnp.float32)
        hr, hi = s5_scan(uf, ab_re, ab_im, bb_re, bb_im, init, reverse)
        y = y + jnp.einsum('gcp,bsgp->bsgc', c_re[dirn].astype(jnp.float32), hr) \
              - jnp.einsum('gcp,bsgp->bsgc', c_im[dirn].astype(jnp.float32), hi)
        if return_state:
            last = 0 if reverse else s - 1
            finals.append(jnp.stack([hr[:, last], hi[:, last]], axis=-1))
    y = y.reshape(bsz, s, S5_CH).astype(u.dtype)
    out = jax.nn.gelu(y) * jax.nn.sigmoid(y @ w_glu + b_glu)
    if return_state:
        return out, jnp.stack(finals, axis=1).astype(u.dtype)
    return out


def s5_params(lp):
    return (lp['s5_lambda_re'], lp['s5_lambda_im'], lp['s5_log_dt'], lp['s5_b_re'], lp['s5_b_im'],
            lp['s5_c_re'], lp['s5_c_im'], lp['s5_d'], lp['s5_w_glu'], lp['s5_b_glu'])


def mixer_context(h, lp):
    b, s, _ = h.shape
    zq, zkv, zkr, u, gq, gk, gv = jnp.split(h @ lp['w_in'], SPLIT_POINTS, axis=-1)
    ckv = rms_norm(zkv, lp['mla_kv_norm'])
    q = mla_queries(zq, lp['mla_q_norm'], lp['mla_w_uq'], None, None)
    k, v = mla_keys_values(ckv, zkr, lp['mla_w_ukv'], None, None)
    o_mla = attention(q, k, v, MLA_SCALE).reshape(b, s, MLA_HEADS * MLA_V)
    y_s5, s5_state = s5_mixer(u, *s5_params(lp), None, True)
    qg = rms_norm(gq.reshape(b, s, GQA_Q_HEADS, GQA_HEAD_DIM), lp['gqa_q_norm'])
    kg = rms_norm(gk.reshape(b, s, GQA_KV_HEADS, GQA_HEAD_DIM), lp['gqa_k_norm'])
    vg = gv.reshape(b, s, GQA_KV_HEADS, GQA_HEAD_DIM)
    o_gqa = attention(qg, kg, vg, GQA_SCALE).reshape(b, s, GQA_Q_HEADS * GQA_HEAD_DIM)
    out = jnp.concatenate([o_mla, y_s5, o_gqa], axis=-1) @ lp['w_out']
    return out, (ckv, zkr, kg, vg, s5_state)


def mixer_latent(h, ctx_ckv, ctx_krope, ctx_k, ctx_v, ctx_s5, lp):
    b, s, _ = h.shape
    cos_m, sin_m = grid_angles(s, MLA_ROPE)
    cos_g, sin_g = grid_angles(s, GQA_HEAD_DIM)
    zq, zkv, zkr, u, gq, gk, gv = jnp.split(h @ lp['w_in'], SPLIT_POINTS, axis=-1)
    q = mla_queries(zq, lp['mla_q_norm'], lp['mla_w_uq'], cos_m, sin_m)
    k_lat, v_lat = mla_keys_values(rms_norm(zkv, lp['mla_kv_norm']), zkr, lp['mla_w_ukv'], cos_m, sin_m)
    k_ctx, v_ctx = mla_keys_values(ctx_ckv, ctx_krope, lp['mla_w_ukv'], None, None)
    o_mla = attention(q, jnp.concatenate([k_ctx, k_lat], axis=1), jnp.concatenate([v_ctx, v_lat], axis=1),
                      MLA_SCALE).reshape(b, s, MLA_HEADS * MLA_V)
    y_s5 = s5_mixer(u, *s5_params(lp), ctx_s5, False)
    qg = apply_rope(rms_norm(gq.reshape(b, s, GQA_Q_HEADS, GQA_HEAD_DIM), lp['gqa_q_norm']), cos_g, sin_g)
    kg = apply_rope(rms_norm(gk.reshape(b, s, GQA_KV_HEADS, GQA_HEAD_DIM), lp['gqa_k_norm']), cos_g, sin_g)
    vg = gv.reshape(b, s, GQA_KV_HEADS, GQA_HEAD_DIM)
    o_gqa = attention(qg, jnp.concatenate([ctx_k, kg], axis=1), jnp.concatenate([ctx_v, vg], axis=1),
                      GQA_SCALE).reshape(b, s, GQA_Q_HEADS * GQA_HEAD_DIM)
    return jnp.concatenate([o_mla, y_s5, o_gqa], axis=-1) @ lp['w_out']


def trunk_layer(x, mods, lp, mix_fn):
    sh1, sc1, g1, sh2, sc2, g2, sh3, sc3, g3 = mods
    x = x + 0.5 * g1 * swiglu(modulate(x, lp['norm_ffn1'], sh1, sc1),
                              lp['ffn1_w_gate'], lp['ffn1_w_up'], lp['ffn1_w_down'])
    mixed, extra = mix_fn(modulate(x, lp['norm_mix'], sh2, sc2))
    x = x + g2 * mixed
    x = x + 0.5 * g3 * swiglu(modulate(x, lp['norm_ffn2'], sh3, sc3),
                              lp['ffn2_w_gate'], lp['ffn2_w_up'], lp['ffn2_w_down'])
    return x, extra


def setup_inputs(seed: int = 0) -> dict:
    key = jax.random.key(seed)
    ks = iter(jax.random.split(key, 64))

    def nrm(shape, std):
        return std * jax.random.normal(next(ks), shape, jnp.float32)

    def gain(shape):
        return 1.0 + nrm(shape, 0.05)

    L, D = DEPTH, D_MODEL
    G, P = S5_GROUPS, S5_STATE
    inp = {}
    inp['x_prompt'] = nrm((BATCH, SEQ, D), 1.0)
    inp['x_sample'] = nrm((DEC_BATCH, DEC_SEQ, D), 1.0)
    inp['cache_mla_ckv'] = nrm((DEC_BATCH, L, PAST_LEN, MLA_KV_RANK), 1.0)
    inp['cache_mla_krope'] = nrm((DEC_BATCH, L, PAST_LEN, MLA_ROPE), 1.0)
    inp['cache_gqa_k'] = nrm((DEC_BATCH, L, PAST_LEN, GQA_KV_HEADS, GQA_HEAD_DIM), 1.0)
    inp['cache_gqa_v'] = nrm((DEC_BATCH, L, PAST_LEN, GQA_KV_HEADS, GQA_HEAD_DIM), 1.0)
    inp['state_s5'] = nrm((DEC_BATCH, L, 2, G, P, 2), 0.3)
    inp['c'] = nrm((DEC_BATCH, D), 1.0)
    inp['c_ctx'] = nrm((D,), 1.0)
    inp['w_ada'] = nrm((L, D, N_MOD * D), 0.5 * D ** -0.5)
    inp['b_ada'] = nrm((L, N_MOD * D), 0.02)
    inp['norm_ffn1'] = gain((L, D))
    inp['ffn1_w_gate'] = nrm((L, D, D_FF), D ** -0.5)
    inp['ffn1_w_up'] = nrm((L, D, D_FF), D ** -0.5)
    inp['ffn1_w_down'] = nrm((L, D_FF, D), D_FF ** -0.5)
    inp['norm_mix'] = gain((L, D))
    inp['w_in'] = nrm((L, D, D_IN), D ** -0.5)
    inp['mla_q_norm'] = gain((L, MLA_Q_RANK))
    inp['mla_w_uq'] = nrm((L, MLA_Q_RANK, MLA_HEADS * (MLA_NOPE + MLA_ROPE)), MLA_Q_RANK ** -0.5)
    inp['mla_kv_norm'] = gain((L, MLA_KV_RANK))
    inp['mla_w_ukv'] = nrm((L, MLA_KV_RANK, MLA_HEADS * (MLA_NOPE + MLA_V)), MLA_KV_RANK ** -0.5)
    inp['s5_lambda_re'] = -0.5 + nrm((L, 2, G, P), 0.01)
    inp['s5_lambda_im'] = math.pi * jnp.arange(P, dtype=jnp.float32) + nrm((L, 2, G, P), 0.01)
    inp['s5_log_dt'] = jax.random.uniform(next(ks), (L, 2, G), jnp.float32, math.log(1e-3), math.log(1e-1))
    inp['s5_b_re'] = nrm((L, 2, G, P, S5_GROUP), (2 * S5_GROUP) ** -0.5)
    inp['s5_b_im'] = nrm((L, 2, G, P, S5_GROUP), (2 * S5_GROUP) ** -0.5)
    inp['s5_c_re'] = nrm((L, 2, G, S5_GROUP, P), (2 * P) ** -0.5)
    inp['s5_c_im'] = nrm((L, 2, G, S5_GROUP, P), (2 * P) ** -0.5)
    inp['s5_d'] = nrm((L, S5_CH), 0.5)
    inp['s5_w_glu'] = nrm((L, S5_CH, S5_CH), S5_CH ** -0.5)
    inp['s5_b_glu'] = nrm((L, S5_CH), 0.02)
    inp['gqa_q_norm'] = gain((L, GQA_HEAD_DIM))
    inp['gqa_k_norm'] = gain((L, GQA_HEAD_DIM))
    inp['w_out'] = nrm((L, D_MIX, D), D_MIX ** -0.5)
    inp['norm_ffn2'] = gain((L, D))
    inp['ffn2_w_gate'] = nrm((L, D, D_FF), D ** -0.5)
    inp['ffn2_w_up'] = nrm((L, D, D_FF), D ** -0.5)
    inp['ffn2_w_down'] = nrm((L, D_FF, D), D_FF ** -0.5)
    inp['norm_final'] = gain((D,))
    return inp


def reference(x_prompt, x_sample, cache_mla_ckv, cache_mla_krope, cache_gqa_k, cache_gqa_v, state_s5, c,
              c_ctx, w_ada, b_ada, norm_ffn1, ffn1_w_gate, ffn1_w_up, ffn1_w_down, norm_mix, w_in,
              mla_q_norm, mla_w_uq, mla_kv_norm, mla_w_ukv, s5_lambda_re, s5_lambda_im, s5_log_dt,
              s5_b_re, s5_b_im, s5_c_re, s5_c_im, s5_d, s5_w_glu, s5_b_glu, gqa_q_norm, gqa_k_norm,
              w_out, norm_ffn2, ffn2_w_gate, ffn2_w_up, ffn2_w_down, norm_final):
    xp = x_prompt
    xs = x_sample
    st_ckv, st_krope, st_k, st_v, st_s5 = [], [], [], [], []
    for l in range(DEPTH):
        lp = {
            'norm_ffn1': norm_ffn1[l], 'ffn1_w_gate': ffn1_w_gate[l], 'ffn1_w_up': ffn1_w_up[l],
            'ffn1_w_down': ffn1_w_down[l], 'norm_mix': norm_mix[l], 'w_in': w_in[l],
            'mla_q_norm': mla_q_norm[l], 'mla_w_uq': mla_w_uq[l], 'mla_kv_norm': mla_kv_norm[l],
            'mla_w_ukv': mla_w_ukv[l], 's5_lambda_re': s5_lambda_re[l], 's5_lambda_im': s5_lambda_im[l],
            's5_log_dt': s5_log_dt[l], 's5_b_re': s5_b_re[l], 's5_b_im': s5_b_im[l], 's5_c_re': s5_c_re[l],
            's5_c_im': s5_c_im[l], 's5_d': s5_d[l], 's5_w_glu': s5_w_glu[l], 's5_b_glu': s5_b_glu[l],
            'gqa_q_norm': gqa_q_norm[l], 'gqa_k_norm': gqa_k_norm[l], 'w_out': w_out[l],
            'norm_ffn2': norm_ffn2[l], 'ffn2_w_gate': ffn2_w_gate[l], 'ffn2_w_up': ffn2_w_up[l],
            'ffn2_w_down': ffn2_w_down[l],
        }
        mods_ctx = modulation(c_ctx, w_ada[l], b_ada[l])
        xp, (ckv, krope, kg, vg, s5s) = trunk_layer(xp, mods_ctx, lp, lambda hm: mixer_context(hm, lp))
        st_ckv.append(ckv)
        st_krope.append(krope)
        st_k.append(kg)
        st_v.append(vg)
        st_s5.append(s5s)
        mods_lat = [m[:, None, :] for m in modulation(c, w_ada[l], b_ada[l])]
        xs, _ = trunk_layer(
            xs, mods_lat, lp,
            lambda hm: (mixer_latent(hm, cache_mla_ckv[:, l], cache_mla_krope[:, l], cache_gqa_k[:, l],
                                     cache_gqa_v[:, l], state_s5[:, l], lp), None))
    y_prompt = rms_norm(xp, norm_final)
    y_sample = rms_norm(xs, norm_final)
    new_mla_ckv = jnp.stack(st_ckv, axis=1)
    new_mla_krope = jnp.stack(st_krope, axis=1)
    new_gqa_k = jnp.stack(st_k, axis=1)
    new_gqa_v = jnp.stack(st_v, axis=1)
    new_state_s5 = jnp.stack(st_s5, axis=1)
    return (y_prompt, y_sample, new_mla_ckv, new_mla_krope, new_gqa_k, new_gqa_v, new_state_s5)
```

```python
import functools
import math

import jax
import jax.numpy as jnp
from jax import lax
from jax.experimental import pallas as pl
from jax.experimental.pallas import tpu as pltpu

D_MODEL = 2048
DEPTH = 4
GRID_W = 64
ROPE_THETA = 10000.0
EPS = 1e-6
N_MOD = 9
D_FF = 2 * D_MODEL

MLA_HEADS = 8
MLA_Q_RANK = D_MODEL // 4
MLA_KV_RANK = D_MODEL // 8
MLA_NOPE = 128
MLA_ROPE = 64
MLA_V = 128
MLA_QK_PAD = 256

S5_CH = D_MODEL // 4
S5_GROUP = 16
S5_GROUPS = S5_CH // S5_GROUP
S5_STATE = 64
S5_NSTATE = S5_GROUPS * S5_STATE

GQA_Q_HEADS = 4
GQA_KV_HEADS = 2
GQA_HEAD_DIM = 128

MLA_SCALE = 1.0 / math.sqrt(MLA_NOPE + MLA_ROPE)
GQA_SCALE = 1.0 / math.sqrt(GQA_HEAD_DIM)

ZQ0, ZKV0, ZKR0, ZU0, ZGQ0, ZGK0, ZGV0, Z_END = 0, 512, 768, 896, 1408, 1920, 2176, 2432

LANE = 128
MOD_ROWS = 16

F32 = jnp.float32
BF16 = jnp.bfloat16

VMEM_LIMIT = 56 * 1024 * 1024


def _cparams(sem):
    return pltpu.CompilerParams(dimension_semantics=sem, vmem_limit_bytes=VMEM_LIMIT)


def _rms(x, g):
    return x * lax.rsqrt(jnp.mean(x * x, axis=-1, keepdims=True) + EPS) * g


def _sigmoid(x):
    return 1.0 / (1.0 + jnp.exp(-x))


def _dot(a, b):
    return jnp.dot(a, b, preferred_element_type=F32)


def _mod_kernel(c_ref, w_ref, b_ref, o_ref):
    c = c_ref[...]
    a = (c * _sigmoid(c)).astype(BF16)
    o_ref[...] = _dot(a, w_ref[...].astype(BF16)) + b_ref[...]


def _modulation(cmat, w_ada, b_ada):
    tn = 1024
    nd = N_MOD * D_MODEL
    out = pl.pallas_call(
        _mod_kernel,
        grid=(DEPTH, nd // tn),
        in_specs=[
            pl.BlockSpec((MOD_ROWS, D_MODEL), lambda l, j: (0, 0)),
            pl.BlockSpec((None, D_MODEL, tn), lambda l, j: (l, 0, j)),
            pl.BlockSpec((None, 1, tn), lambda l, j: (l, 0, j)),
        ],
        out_specs=pl.BlockSpec((None, MOD_ROWS, tn), lambda l, j: (l, 0, j)),
        out_shape=jax.ShapeDtypeStruct((DEPTH, MOD_ROWS, nd), F32),
        compiler_params=_cparams(("parallel", "parallel")),
        name="modulation",
    )(cmat, w_ada, b_ada.reshape(DEPTH, 1, nd))
    return out.reshape(DEPTH, MOD_ROWS, N_MOD, D_MODEL)


def _ffn_kernel(x_ref, mod_ref, g_ref, wg_ref, wu_ref, wd_ref, o_ref, hm_ref, *, k0):
    j = pl.program_id(1)

    @pl.when(j == 0)
    def _():
        x = x_ref[...]
        h = _rms(x, g_ref[...]) * (1.0 + mod_ref[k0 + 1:k0 + 2, :]) + mod_ref[k0:k0 + 1, :]
        hm_ref[...] = h.astype(BF16)
        o_ref[...] = x

    h = hm_ref[...]
    g = _dot(h, wg_ref[...])
    u = _dot(h, wu_ref[...])
    a = (g * _sigmoid(g)) * u
    y = _dot(a.astype(BF16), wd_ref[...])
    o_ref[...] += (0.5 * mod_ref[k0 + 2:k0 + 3, :]) * y


def _ffn(x, mods_l, mod_base, rows_per_mod, norm_g, wg, wu, wd, l, k0):
    n = x.shape[0]
    tm, tf = 512, 512
    return pl.pallas_call(
        functools.partial(_ffn_kernel, k0=k0),
        grid=(n // tm, D_FF // tf),
        in_specs=[
            pl.BlockSpec((tm, D_MODEL), lambda i, j: (i, 0)),
            pl.BlockSpec((None, N_MOD, D_MODEL), lambda i, j: (mod_base + (i * tm) // rows_per_mod, 0, 0)),
            pl.BlockSpec((None, 1, D_MODEL), lambda i, j: (l, 0, 0)),
            pl.BlockSpec((None, D_MODEL, tf), lambda i, j: (l, 0, j)),
            pl.BlockSpec((None, D_MODEL, tf), lambda i, j: (l, 0, j)),
            pl.BlockSpec((None, tf, D_MODEL), lambda i, j: (l, j, 0)),
        ],
        out_specs=pl.BlockSpec((tm, D_MODEL), lambda i, j: (i, 0)),
        out_shape=jax.ShapeDtypeStruct((n, D_MODEL), F32),
        scratch_shapes=[pltpu.VMEM((tm, D_MODEL), BF16)],
        compiler_params=_cparams(("parallel", "arbitrary")),
        name="ffn",
    )(x, mods_l, norm_g, wg, wu, wd)


def _rope_mla(t, cos, sa, sb):
    return t * cos + pltpu.roll(t, 32, 1) * sa + pltpu.roll(t, 96, 1) * sb


def _rope_gqa(t, cos, ss):
    return t * cos + pltpu.roll(t, 64, 1) * ss


def _inproj_kernel(x_ref, mod_ref, g_ref, win_ref, qn_ref, wuq_ref, kvn_ref, wukv_ref, gqn_ref, gkn_ref,
                   cm_ref, sa_ref, sb_ref, cg_ref, sg_ref,
                   qm_ref, km_ref, vm_ref, u_ref, gq_ref, gk_ref, gv_ref, *cache_refs):
    x = x_ref[...]
    h = (_rms(x, g_ref[...]) * (1.0 + mod_ref[4:5, :]) + mod_ref[3:4, :]).astype(BF16)
    z = _dot(h, win_ref[...])
    cm, sa, sb = cm_ref[...], sa_ref[...], sb_ref[...]
    cg, sg = cg_ref[...], sg_ref[...]

    ckv = _rms(z[:, ZKV0:ZKR0], kvn_ref[...])
    kr = _rope_mla(z[:, ZKR0:ZU0], cm, sa, sb)
    kr_b = kr.astype(BF16)

    q = _dot(_rms(z[:, ZQ0:ZKV0], qn_ref[...]).astype(BF16), wuq_ref[...])
    kv = _dot(ckv.astype(BF16), wukv_ref[...])
    for hd in range(MLA_HEADS):
        c0 = hd * MLA_QK_PAD
        qm_ref[:, c0:c0 + LANE] = q[:, c0:c0 + LANE].astype(BF16)
        qm_ref[:, c0 + LANE:c0 + 2 * LANE] = _rope_mla(q[:, c0 + LANE:c0 + 2 * LANE], cm, sa, sb).astype(BF16)
        km_ref[:, c0:c0 + LANE] = kv[:, hd * LANE:(hd + 1) * LANE].astype(BF16)
        km_ref[:, c0 + LANE:c0 + 2 * LANE] = kr_b
    vm_ref[...] = kv[:, MLA_HEADS * MLA_NOPE:].astype(BF16)

    u_ref[...] = z[:, ZU0:ZGQ0]

    for hd in range(GQA_Q_HEADS):
        t = _rms(z[:, ZGQ0 + hd * LANE:ZGQ0 + (hd + 1) * LANE], gqn_ref[...])
        gq_ref[:, hd * LANE:(hd + 1) * LANE] = _rope_gqa(t, cg, sg).astype(BF16)
    kgs = []
    for hd in range(GQA_KV_HEADS):
        t = _rms(z[:, ZGK0 + hd * LANE:ZGK0 + (hd + 1) * LANE], gkn_ref[...])
        t = _rope_gqa(t, cg, sg)
        kgs.append(t)
        gk_ref[:, hd * LANE:(hd + 1) * LANE] = t.astype(BF16)
    gv = z[:, ZGV0:Z_END]
    gv_ref[...] = gv.astype(BF16)

    if cache_refs:
        ckv_o, kr_o, kg_o, vg_o = cache_refs
        ckv_o[...] = ckv
        kr_o[...] = kr[:, :MLA_ROPE]
        for hd in range(GQA_KV_HEADS):
            kg_o[:, hd * LANE:(hd + 1) * LANE] = kgs[hd]
        vg_o[...] = gv


def _inproj(x, bsz, seq, mods_l, mod_base, per_batch_mod, w, l, rope, emit_cache):
    n = bsz * seq
    tm = 256
    nt = seq // tm
    cm, sa, sb, cg, sg = rope
    rope_blocks = cm.shape[0] // tm
    rmap = (lambda b, j: (j, 0)) if rope_blocks > 1 else (lambda b, j: (0, 0))
    row = lambda b, j: (b * nt + j, 0)
    lsel = lambda b, j: (l, 0, 0)
    mod_map = (lambda b, j: (mod_base + b, 0, 0)) if per_batch_mod else (lambda b, j: (mod_base, 0, 0))
    out_shape = [
        jax.ShapeDtypeStruct((n, MLA_HEADS * MLA_QK_PAD), BF16),
        jax.ShapeDtypeStruct((n, MLA_HEADS * MLA_QK_PAD), BF16),
        jax.ShapeDtypeStruct((n, MLA_HEADS * MLA_V), BF16),
        jax.ShapeDtypeStruct((seq, bsz * S5_CH), F32),
        jax.ShapeDtypeStruct((n, GQA_Q_HEADS * GQA_HEAD_DIM), BF16),
        jax.ShapeDtypeStruct((n, GQA_KV_HEADS * GQA_HEAD_DIM), BF16),
        jax.ShapeDtypeStruct((n, GQA_KV_HEADS * GQA_HEAD_DIM), BF16),
    ]
    out_specs = [
        pl.BlockSpec((tm, MLA_HEADS * MLA_QK_PAD), row),
        pl.BlockSpec((tm, MLA_HEADS * MLA_QK_PAD), row),
        pl.BlockSpec((tm, MLA_HEADS * MLA_V), row),
        pl.BlockSpec((tm, S5_CH), lambda b, j: (j, b)),
        pl.BlockSpec((tm, GQA_Q_HEADS * GQA_HEAD_DIM), row),
        pl.BlockSpec((tm, GQA_KV_HEADS * GQA_HEAD_DIM), row),
        pl.BlockSpec((tm, GQA_KV_HEADS * GQA_HEAD_DIM), row),
    ]
    if emit_cache:
        out_shape += [
            jax.ShapeDtypeStruct((n, MLA_KV_RANK), F32),
            jax.ShapeDtypeStruct((n, MLA_ROPE), F32),
            jax.ShapeDtypeStruct((n, GQA_KV_HEADS * GQA_HEAD_DIM), F32),
            jax.ShapeDtypeStruct((n, GQA_KV_HEADS * GQA_HEAD_DIM), F32),
        ]
        out_specs += [
            pl.BlockSpec((tm, MLA_KV_RANK), row),
            pl.BlockSpec((tm, MLA_ROPE), row),
            pl.BlockSpec((tm, GQA_KV_HEADS * GQA_HEAD_DIM), row),
            pl.BlockSpec((tm, GQA_KV_HEADS * GQA_HEAD_DIM), row),
        ]
    return pl.pallas_call(
        _inproj_kernel,
        grid=(bsz, nt),
        in_specs=[
            pl.BlockSpec((tm, D_MODEL), row),
            pl.BlockSpec((None, N_MOD, D_MODEL), mod_map),
            pl.BlockSpec((None, 1, D_MODEL), lsel),
            pl.BlockSpec((None, D_MODEL, Z_END), lsel),
            pl.BlockSpec((None, 1, MLA_Q_RANK), lsel),
            pl.BlockSpec((None, MLA_Q_RANK, MLA_HEADS * MLA_QK_PAD), lsel),
            pl.BlockSpec((None, 1, MLA_KV_RANK), lsel),
            pl.BlockSpec((None, MLA_KV_RANK, MLA_HEADS * (MLA_NOPE + MLA_V)), lsel),
            pl.BlockSpec((None, 1, GQA_HEAD_DIM), lsel),
            pl.BlockSpec((None, 1, GQA_HEAD_DIM), lsel),
        ] + [pl.BlockSpec((tm, LANE), rmap)] * 5,
        out_specs=out_specs,
        out_shape=out_shape,
        compiler_params=_cparams(("parallel", "parallel")),
        name="inproj",
    )(x, mods_l, w["norm_mix"], w["w_in"], w["mla_q_norm"], w["mla_w_uq"], w["mla_kv_norm"], w["mla_w_ukv"],
      w["gqa_q_norm"], w["gqa_k_norm"], cm, sa, sb, cg, sg)


def _kvup_kernel(ckv_ref, kr_ref, wukv_ref, km_ref, vm_ref):
    kv = _dot(ckv_ref[...].astype(BF16), wukv_ref[...])
    kr_b = kr_ref[...].astype(BF16)
    for hd in range(MLA_HEADS):
        c0 = hd * MLA_QK_PAD
        km_ref[:, c0:c0 + LANE] = kv[:, hd * LANE:(hd + 1) * LANE].astype(BF16)
        km_ref[:, c0 + LANE:c0 + 2 * LANE] = kr_b
    vm_ref[...] = kv[:, MLA_HEADS * MLA_NOPE:].astype(BF16)


def _kvup(cache_ckv, cache_kr_pad, wukv, l):
    bsz, _, past, _ = cache_ckv.shape
    n = bsz * past
    return pl.pallas_call(
        _kvup_kernel,
        grid=(bsz,),
        in_specs=[
            pl.BlockSpec((None, None, past, MLA_KV_RANK), lambda b: (b, l, 0, 0)),
            pl.BlockSpec((None, None, past, LANE), lambda b: (b, l, 0, 0)),
            pl.BlockSpec((None, MLA_KV_RANK, MLA_HEADS * (MLA_NOPE + MLA_V)), lambda b: (l, 0, 0)),
        ],
        out_specs=[
            pl.BlockSpec((past, MLA_HEADS * MLA_QK_PAD), lambda b: (b, 0)),
            pl.BlockSpec((past, MLA_HEADS * MLA_V), lambda b: (b, 0)),
        ],
        out_shape=[
            jax.ShapeDtypeStruct((n, MLA_HEADS * MLA_QK_PAD), BF16),
            jax.ShapeDtypeStruct((n, MLA_HEADS * MLA_V), BF16),
        ],
        compiler_params=_cparams(("parallel",)),
        name="kvup",
    )(cache_ckv, cache_kr_pad, wukv)


def _attn_kernel(*refs, nparts, scale):
    q_ref = refs[0]
    k_refs = refs[1:1 + nparts]
    v_refs = refs[1 + nparts:1 + 2 * nparts]
    o_ref = refs[1 + 2 * nparts]
    q = q_ref[...]
    dn = (((1,), (1,)), ((), ()))
    s = [lax.dot_general(q, k[...].astype(BF16), dn, preferred_element_type=F32) * scale for k in k_refs]
    m = functools.reduce(jnp.maximum, [jnp.max(t, axis=-1, keepdims=True) for t in s])
    p = [jnp.exp(t - m) for t in s]
    den = functools.reduce(lambda a, b: a + b, [jnp.sum(t, axis=-1, keepdims=True) for t in p])
    acc = functools.reduce(lambda a, b: a + b,
                           [_dot(t.astype(BF16), v[...].astype(BF16)) for t, v in zip(p, v_refs)])
    o_ref[...] = (acc / den).astype(o_ref.dtype)


def _attention(q, parts, bsz, seq, heads, rep, dk, dv, scale, tq):
    nq = seq // tq
    in_specs = [pl.BlockSpec((tq, dk), lambda b, h, i: (b * nq + i, h))]
    in_specs += [p[2] for p in parts] + [p[3] for p in parts]
    args = [q] + [p[0] for p in parts] + [p[1] for p in parts]
    return pl.pallas_call(
        functools.partial(_attn_kernel, nparts=len(parts), scale=scale),
        grid=(bsz, heads, nq),
        in_specs=in_specs,
        out_specs=pl.BlockSpec((tq, dv), lambda b, h, i: (b * nq + i, h)),
        out_shape=jax.ShapeDtypeStruct((bsz * seq, heads * dv), BF16),
        compiler_params=_cparams(("parallel", "parallel", "arbitrary")),
        name="attention",
    )(*args)


def _kv_part_2d(k, v, rows, dk, dv, rep):
    return (k, v,
            pl.BlockSpec((rows, dk), lambda b, h, i: (b, h // rep)),
            pl.BlockSpec((rows, dv), lambda b, h, i: (b, h // rep)))


def _kv_part_cache(k, v, rows, dk, dv, rep, l):
    return (k, v,
            pl.BlockSpec((None, None, rows, dk), lambda b, h, i: (b, l, 0, h // rep)),
            pl.BlockSpec((None, None, rows, dv), lambda b, h, i: (b, l, 0, h // rep)))


S5_LC = 512
S5_Q = S5_CH // LANE
S5_QS = S5_NSTATE // S5_Q


def _s5_kernel(u_ref, add_ref, bw_ref, cw_ref, ar_ref, ai_ref, h0r_ref, h0i_ref,
               y_ref, fr_ref, fi_ref, hr_ref, hi_ref, sr_ref, si_ref, *, bsz, tt, reverse, scale_add):
    j = pl.program_id(0)

    @pl.when(j == 0)
    def _():
        sr_ref[...] = h0r_ref[...]
        si_ref[...] = h0i_ref[...]

    u = u_ref[...]
    ub = u.astype(BF16)
    for q in range(S5_Q):
        bu = _dot(ub[:, q * LANE:(q + 1) * LANE], bw_ref[q])
        hr_ref[:, q * S5_QS:(q + 1) * S5_QS] = bu[:, :S5_QS]
        hi_ref[:, q * S5_QS:(q + 1) * S5_QS] = bu[:, S5_QS:]

    for c in range(S5_NSTATE // S5_LC):
        lanes = slice(c * S5_LC, (c + 1) * S5_LC)
        ar = jnp.broadcast_to(ar_ref[:, lanes], (bsz, S5_LC))
        ai = jnp.broadcast_to(ai_ref[:, lanes], (bsz, S5_LC))

        def step(t, carry, lanes=lanes, ar=ar, ai=ai):
            pr, pi = carry
            tt_i = (tt - 1 - t) if reverse else t
            r0 = pl.multiple_of(tt_i * bsz, bsz)
            nr = ar * pr - ai * pi + hr_ref[pl.ds(r0, bsz), lanes]
            ni = ar * pi + ai * pr + hi_ref[pl.ds(r0, bsz), lanes]
            hr_ref[pl.ds(r0, bsz), lanes] = nr
            hi_ref[pl.ds(r0, bsz), lanes] = ni
            return nr, ni

        pr, pi = lax.fori_loop(0, tt, step, (sr_ref[:, lanes], si_ref[:, lanes]))
        sr_ref[:, lanes] = pr
        si_ref[:, lanes] = pi

    add = add_ref[...]
    for q in range(S5_Q):
        hs = slice(q * S5_QS, (q + 1) * S5_QS)
        cols = slice(q * LANE, (q + 1) * LANE)
        yq = _dot(hr_ref[:, hs].astype(BF16), cw_ref[q, :S5_QS, :]) + _dot(hi_ref[:, hs].astype(BF16), cw_ref[q, S5_QS:, :])
        if scale_add:
            y_ref[:, cols] = add[:, cols] * u[:, cols] + yq
        else:
            y_ref[:, cols] = add[:, cols] + yq

    fr_ref[...] = sr_ref[...]
    fi_ref[...] = si_ref[...]


def _s5_scan(u_tm, add, bw, cw, ar, ai, h0r, h0i, bsz, seq, tt, reverse, scale_add, l, dirn):
    rows = tt * bsz
    nt = seq // tt
    tmap = (lambda j: (nt - 1 - j, 0)) if reverse else (lambda j: (j, 0))
    wsel = lambda j: (l, dirn, 0, 0, 0)
    asel = lambda j: (l, dirn, 0, 0)
    add_spec = pl.BlockSpec((None, 1, S5_CH), lambda j: (l, 0, 0)) if scale_add else pl.BlockSpec((rows, S5_CH), tmap)
    return pl.pallas_call(
        functools.partial(_s5_kernel, bsz=bsz, tt=tt, reverse=reverse, scale_add=scale_add),
        grid=(nt,),
        in_specs=[
            pl.BlockSpec((rows, S5_CH), tmap),
            add_spec,
            pl.BlockSpec((None, None, S5_Q, LANE, 2 * S5_QS), wsel),
            pl.BlockSpec((None, None, S5_Q, 2 * S5_QS, LANE), wsel),
            pl.BlockSpec((None, None, 1, S5_NSTATE), asel),
            pl.BlockSpec((None, None, 1, S5_NSTATE), asel),
            pl.BlockSpec((bsz, S5_NSTATE), lambda j: (0, 0)),
            pl.BlockSpec((bsz, S5_NSTATE), lambda j: (0, 0)),
        ],
        out_specs=[
            pl.BlockSpec((rows, S5_CH), tmap),
            pl.BlockSpec((bsz, S5_NSTATE), lambda j: (0, 0)),
            pl.BlockSpec((bsz, S5_NSTATE), lambda j: (0, 0)),
        ],
        out_shape=[
            jax.ShapeDtypeStruct((seq * bsz, S5_CH), F32),
            jax.ShapeDtypeStruct((bsz, S5_NSTATE), F32),
            jax.ShapeDtypeStruct((bsz, S5_NSTATE), F32),
        ],
        scratch_shapes=[
            pltpu.VMEM((rows, S5_NSTATE), F32),
            pltpu.VMEM((rows, S5_NSTATE), F32),
            pltpu.VMEM((bsz, S5_NSTATE), F32),
            pltpu.VMEM((bsz, S5_NSTATE), F32),
        ],
        compiler_params=_cparams(("arbitrary",)),
        name="s5_scan",
    )(u_tm, add, bw, cw, ar, ai, h0r, h0i)


def _s5_weights(lam_re, lam_im, log_dt, b_re, b_im, c_re, c_im):
    dt = jnp.exp(log_dt)[..., None]
    mag = jnp.exp(lam_re * dt)
    ab_re = mag * jnp.cos(lam_im * dt)
    ab_im = mag * jnp.sin(lam_im * dt)
    den = lam_re * lam_re + lam_im * lam_im
    nr = ab_re - 1.0
    f_re = (nr * lam_re + ab_im * lam_im) / den
    f_im = (ab_im * lam_re - nr * lam_im) / den
    bb_re = f_re[..., None] * b_re - f_im[..., None] * b_im
    bb_im = f_re[..., None] * b_im + f_im[..., None] * b_re
    eye = jnp.eye(8, dtype=F32)
    lshape = lam_re.shape[:2]

    def pack_b(bb):
        t = bb.reshape(*lshape, S5_Q, 8, S5_STATE, S5_GROUP)
        return jnp.einsum("ldqrpc,rs->ldqrcsp", t, eye).reshape(*lshape, S5_Q, LANE, S5_QS)

    def pack_c(cc):
        t = cc.reshape(*lshape, S5_Q, 8, S5_GROUP, S5_STATE)
        return jnp.einsum("ldqrcp,rs->ldqrpsc", t, eye).reshape(*lshape, S5_Q, S5_QS, LANE)

    bw = jnp.concatenate([pack_b(bb_re), pack_b(bb_im)], axis=-1).astype(BF16)
    cw = jnp.concatenate([pack_c(c_re), -pack_c(c_im)], axis=-2).astype(BF16)
    ar = ab_re.reshape(*lshape, 1, S5_NSTATE)
    ai = ab_im.reshape(*lshape, 1, S5_NSTATE)
    return bw, cw, ar, ai


def _outproj_kernel(x_ref, mod_ref, om_ref, y_ref, og_ref, wglu_ref, bglu_ref, wout_ref, o_ref):
    y = y_ref[...]
    gate = _sigmoid(_dot(y.astype(BF16), wglu_ref[...]) + bglu_ref[...])
    gelu = 0.5 * y * (1.0 + jnp.tanh(math.sqrt(2.0 / math.pi) * (y + 0.044715 * (y * y * y))))
    s5 = (gelu * gate).astype(BF16)
    n_mla = MLA_HEADS * MLA_V
    mixed = (_dot(om_ref[...], wout_ref[0:n_mla, :])
             + _dot(s5, wout_ref[n_mla:n_mla + S5_CH, :])
             + _dot(og_ref[...], wout_ref[n_mla + S5_CH:, :]))
    o_ref[...] = x_ref[...] + mod_ref[5:6, :] * mixed


def _outproj(x, bsz, seq, mods_l, mod_base, per_batch_mod, o_mla, y_tm, o_gqa, w, l):
    n = bsz * seq
    tm = 256
    nt = seq // tm
    row = lambda b, j: (b * nt + j, 0)
    lsel = lambda b, j: (l, 0, 0)
    mod_map = (lambda b, j: (mod_base + b, 0, 0)) if per_batch_mod else (lambda b, j: (mod_base, 0, 0))
    return pl.pallas_call(
        _outproj_kernel,
        grid=(bsz, nt),
        in_specs=[
            pl.BlockSpec((tm, D_MODEL), row),
            pl.BlockSpec((None, N_MOD, D_MODEL), mod_map),
            pl.BlockSpec((tm, MLA_HEADS * MLA_V), row),
            pl.BlockSpec((tm, S5_CH), lambda b, j: (j, b)),
            pl.BlockSpec((tm, GQA_Q_HEADS * GQA_HEAD_DIM), row),
            pl.BlockSpec((None, S5_CH, S5_CH), lsel),
            pl.BlockSpec((None, 1, S5_CH), lsel),
            pl.BlockSpec((None, D_MODEL, D_MODEL), lsel),
        ],
        out_specs=pl.BlockSpec((tm, D_MODEL), row),
        out_shape=jax.ShapeDtypeStruct((n, D_MODEL), F32),
        compiler_params=_cparams(("parallel", "parallel")),
        name="outproj",
    )(x, mods_l, o_mla, y_tm, o_gqa, w["s5_w_glu"], w["s5_b_glu"], w["w_out"])


def _norm_kernel(x_ref, g_ref, o_ref):
    o_ref[...] = _rms(x_ref[...], g_ref[...])


def _final_norm(x, g):
    n = x.shape[0]
    tm = 1024
    return pl.pallas_call(
        _norm_kernel,
        grid=(n // tm,),
        in_specs=[pl.BlockSpec((tm, D_MODEL), lambda i: (i, 0)), pl.BlockSpec((1, D_MODEL), lambda i: (0, 0))],
        out_specs=pl.BlockSpec((tm, D_MODEL), lambda i: (i, 0)),
        out_shape=jax.ShapeDtypeStruct((n, D_MODEL), F32),
        compiler_params=_cparams(("parallel",)),
        name="final_norm",
    )(x, g)


def _rope_tables(seq, identity):
    if identity:
        one = jnp.ones((seq, LANE), F32)
        zero = jnp.zeros((seq, LANE), F32)
        return one, zero, zero, one, zero
    n_rows = seq // GRID_W
    row = jnp.broadcast_to(jnp.arange(n_rows, dtype=F32)[:, None], (n_rows, GRID_W)).reshape(seq)
    col = jnp.broadcast_to(jnp.arange(GRID_W, dtype=F32)[None, :], (n_rows, GRID_W)).reshape(seq)

    def angles(rot_dim):
        n_freq = rot_dim // 4
        inv = ROPE_THETA ** (-jnp.arange(n_freq, dtype=F32) / n_freq)
        ang = jnp.concatenate([row[:, None] * inv, col[:, None] * inv], axis=-1)
        return jnp.cos(ang), jnp.sin(ang)

    cm, sm = angles(MLA_ROPE)
    zero32 = jnp.zeros_like(sm)
    pad = jnp.zeros((seq, LANE - MLA_ROPE), F32)
    cos_m = jnp.concatenate([cm, cm, jnp.ones_like(pad)], axis=-1)
    sa = jnp.concatenate([zero32, sm, pad], axis=-1)
    sb = jnp.concatenate([-sm, zero32, pad], axis=-1)
    cg, sg = angles(GQA_HEAD_DIM)
    cos_g = jnp.concatenate([cg, cg], axis=-1)
    sin_g = jnp.concatenate([-sg, sg], axis=-1)
    return cos_m, sa, sb, cos_g, sin_g


def kernel(x_prompt, x_sample, cache_mla_ckv, cache_mla_krope, cache_gqa_k, cache_gqa_v, state_s5, c, c_ctx, w_ada, b_ada, norm_ffn1, ffn1_w_gate, ffn1_w_up, ffn1_w_down, norm_mix, w_in, mla_q_norm, mla_w_uq, mla_kv_norm, mla_w_ukv, s5_lambda_re, s5_lambda_im, s5_log_dt, s5_b_re, s5_b_im, s5_c_re, s5_c_im, s5_d, s5_w_glu, s5_b_glu, gqa_q_norm, gqa_k_norm, w_out, norm_ffn2, ffn2_w_gate, ffn2_w_up, ffn2_w_down, norm_final):
    L = DEPTH
    cb, cs, _ = x_prompt.shape
    lb, ls, _ = x_sample.shape
    past = cache_mla_ckv.shape[2]

    zpad = jnp.zeros((L, D_MODEL, LANE - MLA_ROPE), F32)
    w_in_p = jnp.concatenate([w_in[:, :, :MLA_Q_RANK + MLA_KV_RANK + MLA_ROPE], zpad,
                              w_in[:, :, MLA_Q_RANK + MLA_KV_RANK + MLA_ROPE:]], axis=-1).astype(BF16)
    wuq = mla_w_uq.reshape(L, MLA_Q_RANK, MLA_HEADS, MLA_NOPE + MLA_ROPE)
    wuq = jnp.pad(wuq, ((0, 0), (0, 0), (0, 0), (0, MLA_QK_PAD - MLA_NOPE - MLA_ROPE)))
    wuq = wuq.reshape(L, MLA_Q_RANK, MLA_HEADS * MLA_QK_PAD).astype(BF16)
    wukv = mla_w_ukv.reshape(L, MLA_KV_RANK, MLA_HEADS, MLA_NOPE + MLA_V)
    wukv = jnp.concatenate([wukv[..., :MLA_NOPE].reshape(L, MLA_KV_RANK, MLA_HEADS * MLA_NOPE),
                            wukv[..., MLA_NOPE:].reshape(L, MLA_KV_RANK, MLA_HEADS * MLA_V)], axis=-1).astype(BF16)
    w = {
        "norm_mix": norm_mix.reshape(L, 1, D_MODEL),
        "w_in": w_in_p,
        "mla_q_norm": mla_q_norm.reshape(L, 1, MLA_Q_RANK),
        "mla_w_uq": wuq,
        "mla_kv_norm": mla_kv_norm.reshape(L, 1, MLA_KV_RANK),
        "mla_w_ukv": wukv,
        "gqa_q_norm": gqa_q_norm.reshape(L, 1, GQA_HEAD_DIM),
        "gqa_k_norm": gqa_k_norm.reshape(L, 1, GQA_HEAD_DIM),
        "s5_w_glu": s5_w_glu.astype(BF16),
        "s5_b_glu": s5_b_glu.reshape(L, 1, S5_CH),
        "w_out": w_out.astype(BF16),
    }
    f1 = (norm_ffn1.reshape(L, 1, D_MODEL), ffn1_w_gate.astype(BF16), ffn1_w_up.astype(BF16), ffn1_w_down.astype(BF16))
    f2 = (norm_ffn2.reshape(L, 1, D_MODEL), ffn2_w_gate.astype(BF16), ffn2_w_up.astype(BF16), ffn2_w_down.astype(BF16))
    bw, cw, ar, ai = _s5_weights(s5_lambda_re, s5_lambda_im, s5_log_dt, s5_b_re, s5_b_im, s5_c_re, s5_c_im)
    d_skip = s5_d.reshape(L, 1, S5_CH)
    rope_ctx = _rope_tables(256, True)
    rope_lat = _rope_tables(ls, False)
    cache_kr_pad = jnp.pad(cache_mla_krope, ((0, 0), (0, 0), (0, 0), (0, LANE - MLA_ROPE)))
    cache_k = cache_gqa_k.reshape(lb, L, past, GQA_KV_HEADS * GQA_HEAD_DIM)
    cache_v = cache_gqa_v.reshape(lb, L, past, GQA_KV_HEADS * GQA_HEAD_DIM)
    h0 = state_s5.reshape(lb, L, 2, S5_NSTATE, 2)
    zero_state = jnp.zeros((cb, S5_NSTATE), F32)

    cmat = jnp.concatenate([c_ctx[None, :], c, jnp.zeros((MOD_ROWS - 1 - lb, D_MODEL), F32)], axis=0)
    mods = _modulation(cmat, w_ada, b_ada)

    xp = x_prompt.reshape(cb * cs, D_MODEL)
    xs = x_sample.reshape(lb * ls, D_MODEL)
    st_ckv, st_kr, st_k, st_v, st_s5 = [], [], [], [], []

    def mixer(x, bsz, seq, mods_l, mod_base, per_batch_mod, l, rope, is_ctx):
        outs = _inproj(x, bsz, seq, mods_l, mod_base, per_batch_mod, w, l, rope, is_ctx)
        qm, km, vm, u_tm, gq, gk, gv = outs[:7]
        mrep, grep = 1, GQA_Q_HEADS // GQA_KV_HEADS
        mla_parts = [_kv_part_2d(km, vm, seq, MLA_QK_PAD, MLA_V, mrep)]
        gqa_parts = [_kv_part_2d(gk, gv, seq, GQA_HEAD_DIM, GQA_HEAD_DIM, grep)]
        if is_ctx:
            h0r = h0i = (zero_state, zero_state)
            tq, tt = 256, 32
        else:
            km_c, vm_c = _kvup(cache_mla_ckv, cache_kr_pad, w["mla_w_ukv"], l)
            mla_parts = [_kv_part_2d(km_c, vm_c, past, MLA_QK_PAD, MLA_V, mrep)] + mla_parts
            gqa_parts = [_kv_part_cache(cache_k, cache_v, past, GQA_HEAD_DIM, GQA_HEAD_DIM, grep, l)] + gqa_parts
            h0r = (h0[:, l, 0, :, 0], h0[:, l, 1, :, 0])
            h0i = (h0[:, l, 0, :, 1], h0[:, l, 1, :, 1])
            tq, tt = 512, 64
        o_mla = _attention(qm, mla_parts, bsz, seq, MLA_HEADS, mrep, MLA_QK_PAD, MLA_V, MLA_SCALE, tq)
        o_gqa = _attention(gq, gqa_parts, bsz, seq, GQA_Q_HEADS, grep, GQA_HEAD_DIM, GQA_HEAD_DIM, GQA_SCALE, tq)
        u2 = u_tm.reshape(seq * bsz, S5_CH)
        y_f, ffr, ffi = _s5_scan(u2, d_skip, bw, cw, ar, ai, h0r[0], h0i[0], bsz, seq, tt, False, True, l, 0)
        y, fbr, fbi = _s5_scan(u2, y_f, bw, cw, ar, ai, h0r[1], h0i[1], bsz, seq, tt, True, False, l, 1)
        x = _outproj(x, bsz, seq, mods_l, mod_base, per_batch_mod, o_mla, y.reshape(seq, bsz * S5_CH), o_gqa, w, l)
        extra = None
        if is_ctx:
            ckv, kr, kg, vg = outs[7:]
            s5s = jnp.stack([jnp.stack([ffr, ffi], axis=-1), jnp.stack([fbr, fbi], axis=-1)], axis=1)
            extra = (ckv.reshape(bsz, seq, MLA_KV_RANK), kr.reshape(bsz, seq, MLA_ROPE),
                     kg.reshape(bsz, seq, GQA_KV_HEADS, GQA_HEAD_DIM), vg.reshape(bsz, seq, GQA_KV_HEADS, GQA_HEAD_DIM),
                     s5s.reshape(bsz, 2, S5_GROUPS, S5_STATE, 2))
        return x, extra

    for l in range(L):
        mods_l = mods[l]
        xp = _ffn(xp, mods_l, 0, cb * cs, *f1, l, 0)
        xp, (ckv, kr, kg, vg, s5s) = mixer(xp, cb, cs, mods_l, 0, False, l, rope_ctx, True)
        xp = _ffn(xp, mods_l, 0, cb * cs, *f2, l, 6)
        st_ckv.append(ckv)
        st_kr.append(kr)
        st_k.append(kg)
        st_v.append(vg)
        st_s5.append(s5s)
        xs = _ffn(xs, mods_l, 1, ls, *f1, l, 0)
        xs, _ = mixer(xs, lb, ls, mods_l, 1, True, l, rope_lat, False)
        xs = _ffn(xs, mods_l, 1, ls, *f2, l, 6)

    g_final = norm_final.reshape(1, D_MODEL)
    y_prompt = _final_norm(xp, g_final).reshape(cb, cs, D_MODEL)
    y_sample = _final_norm(xs, g_final).reshape(lb, ls, D_MODEL)
    return (y_prompt, y_sample, jnp.stack(st_ckv, axis=1), jnp.stack(st_kr, axis=1), jnp.stack(st_k, axis=1),
            jnp.stack(st_v, axis=1), jnp.stack(st_s5, axis=1))
```

```python
import functools
import math

import jax
import jax.numpy as jnp
from jax import lax
from jax.experimental import pallas as pl
from jax.experimental.pallas import tpu as pltpu

D_MODEL = 2048
DEPTH = 4
GRID_W = 64
ROPE_THETA = 10000.0
EPS = 1e-6
N_MOD = 9
D_FF = 2 * D_MODEL

MLA_HEADS = 8
MLA_Q_RANK = D_MODEL // 4
MLA_KV_RANK = D_MODEL // 8
MLA_NOPE = 128
MLA_ROPE = 64
MLA_V = 128
MLA_QK_PAD = 256

S5_CH = D_MODEL // 4
S5_GROUP = 16
S5_GROUPS = S5_CH // S5_GROUP
S5_STATE = 64
S5_NSTATE = S5_GROUPS * S5_STATE

GQA_Q_HEADS = 4
GQA_KV_HEADS = 2
GQA_HEAD_DIM = 128

MLA_SCALE = 1.0 / math.sqrt(MLA_NOPE + MLA_ROPE)
GQA_SCALE = 1.0 / math.sqrt(GQA_HEAD_DIM)
LOG2E = math.log2(math.e)
MLA_QSCALE = MLA_SCALE * LOG2E
GQA_QSCALE = GQA_SCALE * LOG2E

ZQ0, ZKV0, ZKR0, ZU0, ZGQ0, ZGK0, ZGV0, Z_END = 0, 512, 768, 896, 1408, 1920, 2176, 2432

LANE = 128
SUBLANE = 8
MOD_ROWS = 16

F32 = jnp.float32
BF16 = jnp.bfloat16

VMEM_LIMIT = 56 * 1024 * 1024


def _cparams(sem):
    return pltpu.CompilerParams(dimension_semantics=sem, vmem_limit_bytes=VMEM_LIMIT)


def _rms(x, g):
    return x * lax.rsqrt(jnp.mean(x * x, axis=-1, keepdims=True) + EPS) * g


def _sigmoid(x):
    return 1.0 / (1.0 + jnp.exp(-x))


def _dot(a, b):
    return jnp.dot(a, b, preferred_element_type=F32)


def _mod_kernel(c_ref, w_ref, b_ref, o_ref):
    c = c_ref[...]
    a = (c * _sigmoid(c)).astype(BF16)
    o_ref[...] = _dot(a, w_ref[...].astype(BF16)) + b_ref[...]


def _modulation(cmat, w_ada, b_ada):
    tn = 1024
    nd = N_MOD * D_MODEL
    out = pl.pallas_call(
        _mod_kernel,
        grid=(DEPTH, nd // tn),
        in_specs=[
            pl.BlockSpec((MOD_ROWS, D_MODEL), lambda l, j: (0, 0)),
            pl.BlockSpec((None, D_MODEL, tn), lambda l, j: (l, 0, j)),
            pl.BlockSpec((None, 1, tn), lambda l, j: (l, 0, j)),
        ],
        out_specs=pl.BlockSpec((None, MOD_ROWS, tn), lambda l, j: (l, 0, j)),
        out_shape=jax.ShapeDtypeStruct((DEPTH, MOD_ROWS, nd), F32),
        compiler_params=_cparams(("parallel", "parallel")),
        name="modulation",
    )(cmat, w_ada, b_ada.reshape(DEPTH, 1, nd))
    return out.reshape(DEPTH, MOD_ROWS, N_MOD, D_MODEL)


def _ffn_kernel(x_ref, mod_ref, g_ref, wg_ref, wu_ref, wd_ref, o_ref, hm_ref, *, k0):
    j = pl.program_id(1)

    @pl.when(j == 0)
    def _():
        x = x_ref[...]
        h = _rms(x, g_ref[...]) * (1.0 + mod_ref[k0 + 1:k0 + 2, :]) + mod_ref[k0:k0 + 1, :]
        hm_ref[...] = h.astype(BF16)
        o_ref[...] = x

    h = hm_ref[...]
    y = None
    for c0 in range(0, wg_ref.shape[1], FFN_SUB):
        g = _dot(h, wg_ref[:, c0:c0 + FFN_SUB])
        u = _dot(h, wu_ref[:, c0:c0 + FFN_SUB])
        a = (g * _sigmoid(g)) * u
        d = _dot(a.astype(BF16), wd_ref[c0:c0 + FFN_SUB, :])
        y = d if y is None else y + d
    o_ref[...] += (0.5 * mod_ref[k0 + 2:k0 + 3, :]) * y


FFN_SUB = 512


def _ffn(x, mods_l, mod_base, rows_per_mod, norm_g, wg, wu, wd, l, k0):
    n = x.shape[0]
    tm, tf = 512, 1024
    return pl.pallas_call(
        functools.partial(_ffn_kernel, k0=k0),
        grid=(n // tm, D_FF // tf),
        in_specs=[
            pl.BlockSpec((tm, D_MODEL), lambda i, j: (i, 0)),
            pl.BlockSpec((None, N_MOD, D_MODEL), lambda i, j: (mod_base + (i * tm) // rows_per_mod, 0, 0)),
            pl.BlockSpec((None, 1, D_MODEL), lambda i, j: (l, 0, 0)),
            pl.BlockSpec((None, D_MODEL, tf), lambda i, j: (l, 0, j)),
            pl.BlockSpec((None, D_MODEL, tf), lambda i, j: (l, 0, j)),
            pl.BlockSpec((None, tf, D_MODEL), lambda i, j: (l, j, 0)),
        ],
        out_specs=pl.BlockSpec((tm, D_MODEL), lambda i, j: (i, 0)),
        out_shape=jax.ShapeDtypeStruct((n, D_MODEL), F32),
        scratch_shapes=[pltpu.VMEM((tm, D_MODEL), BF16)],
        compiler_params=_cparams(("parallel", "arbitrary")),
        name="ffn",
    )(x, mods_l, norm_g, wg, wu, wd)


def _rope_mla(t, cos, sa, sb):
    return t * cos + pltpu.roll(t, 32, 1) * sa + pltpu.roll(t, 96, 1) * sb


def _rope_gqa(t, cos, ss):
    return t * cos + pltpu.roll(t, 64, 1) * ss


def _inproj_kernel(x_ref, mod_ref, g_ref, win_ref, qn_ref, wuq_ref, kvn_ref, wukv_ref, gqn_ref, gkn_ref,
                   cm_ref, sa_ref, sb_ref, cg_ref, sg_ref,
                   qm_ref, km_ref, vm_ref, u_ref, gq_ref, gk_ref, gv_ref, *cache_refs):
    x = x_ref[...]
    h = (_rms(x, g_ref[...]) * (1.0 + mod_ref[4:5, :]) + mod_ref[3:4, :]).astype(BF16)
    z = _dot(h, win_ref[...])
    cm, sa, sb = cm_ref[...], sa_ref[...], sb_ref[...]
    cg, sg = cg_ref[...], sg_ref[...]

    ckv = _rms(z[:, ZKV0:ZKR0], kvn_ref[...])
    kr = _rope_mla(z[:, ZKR0:ZU0], cm, sa, sb)
    kr_b = kr.astype(BF16)

    q = _dot(_rms(z[:, ZQ0:ZKV0], qn_ref[...]).astype(BF16), wuq_ref[...])
    kv = _dot(ckv.astype(BF16), wukv_ref[...])
    for hd in range(MLA_HEADS):
        c0 = hd * MLA_QK_PAD
        qm_ref[:, c0:c0 + LANE] = (q[:, c0:c0 + LANE] * MLA_QSCALE).astype(BF16)
        qm_ref[:, c0 + LANE:c0 + 2 * LANE] = (
            _rope_mla(q[:, c0 + LANE:c0 + 2 * LANE], cm, sa, sb) * MLA_QSCALE).astype(BF16)
        km_ref[:, c0:c0 + LANE] = kv[:, hd * LANE:(hd + 1) * LANE].astype(BF16)
        km_ref[:, c0 + LANE:c0 + 2 * LANE] = kr_b
    vm_ref[...] = kv[:, MLA_HEADS * MLA_NOPE:].T.astype(BF16)

    u_ref[...] = z[:, ZU0:ZGQ0]

    for hd in range(GQA_Q_HEADS):
        t = _rms(z[:, ZGQ0 + hd * LANE:ZGQ0 + (hd + 1) * LANE], gqn_ref[...])
        gq_ref[:, hd * LANE:(hd + 1) * LANE] = (_rope_gqa(t, cg, sg) * GQA_QSCALE).astype(BF16)
    kgs = []
    for hd in range(GQA_KV_HEADS):
        t = _rms(z[:, ZGK0 + hd * LANE:ZGK0 + (hd + 1) * LANE], gkn_ref[...])
        t = _rope_gqa(t, cg, sg)
        kgs.append(t)
        gk_ref[:, hd * LANE:(hd + 1) * LANE] = t.astype(BF16)
    gv = z[:, ZGV0:Z_END]
    gv_ref[...] = gv.T.astype(BF16)

    if cache_refs:
        ckv_o, kr_o, kg_o, vg_o = cache_refs
        ckv_o[...] = ckv
        kr_o[...] = kr[:, :MLA_ROPE]
        for hd in range(GQA_KV_HEADS):
            kg_o[:, hd * LANE:(hd + 1) * LANE] = kgs[hd]
        vg_o[...] = gv


def _inproj(x, bsz, seq, mods_l, mod_base, per_batch_mod, w, l, rope, emit_cache):
    n = bsz * seq
    tm = 256
    nt = seq // tm
    cm, sa, sb, cg, sg = rope
    rope_blocks = cm.shape[0] // tm
    rmap = (lambda b, j: (j, 0)) if rope_blocks > 1 else (lambda b, j: (0, 0))
    row = lambda b, j: (b * nt + j, 0)
    lsel = lambda b, j: (l, 0, 0)
    mod_map = (lambda b, j: (mod_base + b, 0, 0)) if per_batch_mod else (lambda b, j: (mod_base, 0, 0))
    out_shape = [
        jax.ShapeDtypeStruct((n, MLA_HEADS * MLA_QK_PAD), BF16),
        jax.ShapeDtypeStruct((n, MLA_HEADS * MLA_QK_PAD), BF16),
        jax.ShapeDtypeStruct((bsz * MLA_HEADS * MLA_V, seq), BF16),
        jax.ShapeDtypeStruct((seq, bsz * S5_CH), F32),
        jax.ShapeDtypeStruct((n, GQA_Q_HEADS * GQA_HEAD_DIM), BF16),
        jax.ShapeDtypeStruct((n, GQA_KV_HEADS * GQA_HEAD_DIM), BF16),
        jax.ShapeDtypeStruct((bsz * GQA_KV_HEADS * GQA_HEAD_DIM, seq), BF16),
    ]
    out_specs = [
        pl.BlockSpec((tm, MLA_HEADS * MLA_QK_PAD), row),
        pl.BlockSpec((tm, MLA_HEADS * MLA_QK_PAD), row),
        pl.BlockSpec((MLA_HEADS * MLA_V, tm), lambda b, j: (b, j)),
        pl.BlockSpec((tm, S5_CH), lambda b, j: (j, b)),
        pl.BlockSpec((tm, GQA_Q_HEADS * GQA_HEAD_DIM), row),
        pl.BlockSpec((tm, GQA_KV_HEADS * GQA_HEAD_DIM), row),
        pl.BlockSpec((GQA_KV_HEADS * GQA_HEAD_DIM, tm), lambda b, j: (b, j)),
    ]
    if emit_cache:
        out_shape += [
            jax.ShapeDtypeStruct((n, MLA_KV_RANK), F32),
            jax.ShapeDtypeStruct((n, MLA_ROPE), F32),
            jax.ShapeDtypeStruct((n, GQA_KV_HEADS * GQA_HEAD_DIM), F32),
            jax.ShapeDtypeStruct((n, GQA_KV_HEADS * GQA_HEAD_DIM), F32),
        ]
        out_specs += [
            pl.BlockSpec((tm, MLA_KV_RANK), row),
            pl.BlockSpec((tm, MLA_ROPE), row),
            pl.BlockSpec((tm, GQA_KV_HEADS * GQA_HEAD_DIM), row),
            pl.BlockSpec((tm, GQA_KV_HEADS * GQA_HEAD_DIM), row),
        ]
    return pl.pallas_call(
        _inproj_kernel,
        grid=(bsz, nt),
        in_specs=[
            pl.BlockSpec((tm, D_MODEL), row),
            pl.BlockSpec((None, N_MOD, D_MODEL), mod_map),
            pl.BlockSpec((None, 1, D_MODEL), lsel),
            pl.BlockSpec((None, D_MODEL, Z_END), lsel),
            pl.BlockSpec((None, 1, MLA_Q_RANK), lsel),
            pl.BlockSpec((None, MLA_Q_RANK, MLA_HEADS * MLA_QK_PAD), lsel),
            pl.BlockSpec((None, 1, MLA_KV_RANK), lsel),
            pl.BlockSpec((None, MLA_KV_RANK, MLA_HEADS * (MLA_NOPE + MLA_V)), lsel),
            pl.BlockSpec((None, 1, GQA_HEAD_DIM), lsel),
            pl.BlockSpec((None, 1, GQA_HEAD_DIM), lsel),
        ] + [pl.BlockSpec((tm, LANE), rmap)] * 5,
        out_specs=out_specs,
        out_shape=out_shape,
        compiler_params=_cparams(("parallel", "parallel")),
        name="inproj",
    )(x, mods_l, w["norm_mix"], w["w_in"], w["mla_q_norm"], w["mla_w_uq"], w["mla_kv_norm"], w["mla_w_ukv"],
      w["gqa_q_norm"], w["gqa_k_norm"], cm, sa, sb, cg, sg)


def _kvup_kernel(ckv_ref, kr_ref, wukv_ref, km_ref, vm_ref):
    kv = _dot(ckv_ref[...].astype(BF16), wukv_ref[...])
    kr_b = kr_ref[...].astype(BF16)
    for hd in range(MLA_HEADS):
        c0 = hd * MLA_QK_PAD
        km_ref[:, c0:c0 + LANE] = kv[:, hd * LANE:(hd + 1) * LANE].astype(BF16)
        km_ref[:, c0 + LANE:c0 + 2 * LANE] = kr_b
    vm_ref[...] = kv[:, MLA_HEADS * MLA_NOPE:].T.astype(BF16)


def _kvup(cache_ckv, cache_kr_pad, wukv, l):
    bsz, _, past, _ = cache_ckv.shape
    n = bsz * past
    return pl.pallas_call(
        _kvup_kernel,
        grid=(bsz,),
        in_specs=[
            pl.BlockSpec((None, None, past, MLA_KV_RANK), lambda b: (b, l, 0, 0)),
            pl.BlockSpec((None, None, past, LANE), lambda b: (b, l, 0, 0)),
            pl.BlockSpec((None, MLA_KV_RANK, MLA_HEADS * (MLA_NOPE + MLA_V)), lambda b: (l, 0, 0)),
        ],
        out_specs=[
            pl.BlockSpec((past, MLA_HEADS * MLA_QK_PAD), lambda b: (b, 0)),
            pl.BlockSpec((MLA_HEADS * MLA_V, past), lambda b: (b, 0)),
        ],
        out_shape=[
            jax.ShapeDtypeStruct((n, MLA_HEADS * MLA_QK_PAD), BF16),
            jax.ShapeDtypeStruct((bsz * MLA_HEADS * MLA_V, past), BF16),
        ],
        compiler_params=_cparams(("parallel",)),
        name="kvup",
    )(cache_ckv, cache_kr_pad, wukv)


ATTN_CK = 512


def _attn_kernel(*refs, nparts):
    q_ref = refs[0]
    k_refs = refs[1:1 + nparts]
    vt_refs = refs[1 + nparts:1 + 2 * nparts]
    o_ref = refs[1 + 2 * nparts]
    s_ref = refs[2 + 2 * nparts]
    q = q_ref[...]
    tq = q.shape[0]
    dn = (((1,), (1,)), ((), ()))

    chunks = []
    mpart = None
    off = 0
    for pi, k_ref in enumerate(k_refs):
        rows = k_ref.shape[0]
        for r0 in range(0, rows, ATTN_CK):
            ck = min(ATTN_CK, rows - r0)
            s = lax.dot_general(k_ref[r0:r0 + ck, :].astype(BF16), q, dn, preferred_element_type=F32)
            s_ref[off:off + ck, :] = s
            blk = jnp.max(s.reshape(ck // SUBLANE, SUBLANE, tq), axis=0)
            mpart = blk if mpart is None else jnp.maximum(mpart, blk)
            chunks.append((pi, r0, ck, off))
            off += ck
    m = jnp.max(mpart, axis=0, keepdims=True)

    lpart = jnp.zeros(mpart.shape, F32)
    acc = jnp.zeros((vt_refs[0].shape[0], tq), F32)
    for pi, r0, ck, off in chunks:
        p = jnp.exp2(s_ref[off:off + ck, :] - m)
        lpart = lpart + jnp.sum(p.reshape(ck // SUBLANE, SUBLANE, tq), axis=0)
        acc = acc + _dot(vt_refs[pi][:, r0:r0 + ck].astype(BF16), p.astype(BF16))
    den = jnp.sum(lpart, axis=0, keepdims=True)
    o_ref[...] = (acc / den).T.astype(o_ref.dtype)


def _attention(q, parts, bsz, seq, heads, dk, dv, tq):
    nq = seq // tq
    n_keys = sum(p[4] for p in parts)
    assert all(p[4] % LANE == 0 for p in parts)
    in_specs = [pl.BlockSpec((tq, dk), lambda b, h, i: (b * nq + i, h))]
    in_specs += [p[2] for p in parts] + [p[3] for p in parts]
    args = [q] + [p[0] for p in parts] + [p[1] for p in parts]
    return pl.pallas_call(
        functools.partial(_attn_kernel, nparts=len(parts)),
        grid=(bsz, heads, nq),
        in_specs=in_specs,
        out_specs=pl.BlockSpec((tq, dv), lambda b, h, i: (b * nq + i, h)),
        out_shape=jax.ShapeDtypeStruct((bsz * seq, heads * dv), BF16),
        scratch_shapes=[pltpu.VMEM((n_keys, tq), F32)],
        compiler_params=_cparams(("parallel", "parallel", "arbitrary")),
        name="attention",
    )(*args)


def _kv_part_2d(k, vt, rows, dk, dv, kv_heads, rep):
    return (k, vt,
            pl.BlockSpec((rows, dk), lambda b, h, i: (b, h // rep)),
            pl.BlockSpec((dv, rows), lambda b, h, i: (b * kv_heads + h // rep, 0)), rows)


def _kv_part_cache(k, vt, rows, dk, dv, rep, l):
    return (k, vt,
            pl.BlockSpec((None, None, rows, dk), lambda b, h, i: (b, l, 0, h // rep)),
            pl.BlockSpec((None, None, dv, rows), lambda b, h, i: (b, l, h // rep, 0)), rows)


S5_LC = 512
S5_Q = S5_CH // LANE
S5_QS = S5_NSTATE // S5_Q


def _s5_kernel(u_ref, add_ref, bw_ref, cw_ref, ar_ref, ai_ref, h0r_ref, h0i_ref,
               y_ref, fr_ref, fi_ref, hr_ref, hi_ref, sr_ref, si_ref, *, bsz, tt, reverse, scale_add):
    j = pl.program_id(0)

    @pl.when(j == 0)
    def _():
        sr_ref[...] = h0r_ref[...]
        si_ref[...] = h0i_ref[...]

    u = u_ref[...]
    ub = u.astype(BF16)
    for q in range(S5_Q):
        bu = _dot(ub[:, q * LANE:(q + 1) * LANE], bw_ref[q])
        hr_ref[:, q * S5_QS:(q + 1) * S5_QS] = bu[:, :S5_QS]
        hi_ref[:, q * S5_QS:(q + 1) * S5_QS] = bu[:, S5_QS:]

    for c in range(S5_NSTATE // S5_LC):
        lanes = slice(c * S5_LC, (c + 1) * S5_LC)
        ar = jnp.broadcast_to(ar_ref[:, lanes], (bsz, S5_LC))
        ai = jnp.broadcast_to(ai_ref[:, lanes], (bsz, S5_LC))

        def step(t, carry, lanes=lanes, ar=ar, ai=ai):
            pr, pi = carry
            tt_i = (tt - 1 - t) if reverse else t
            r0 = pl.multiple_of(tt_i * bsz, bsz)
            nr = ar * pr - ai * pi + hr_ref[pl.ds(r0, bsz), lanes]
            ni = ar * pi + ai * pr + hi_ref[pl.ds(r0, bsz), lanes]
            hr_ref[pl.ds(r0, bsz), lanes] = nr
            hi_ref[pl.ds(r0, bsz), lanes] = ni
            return nr, ni

        pr, pi = lax.fori_loop(0, tt, step, (sr_ref[:, lanes], si_ref[:, lanes]))
        sr_ref[:, lanes] = pr
        si_ref[:, lanes] = pi

    add = add_ref[...]
    for q in range(S5_Q):
        hs = slice(q * S5_QS, (q + 1) * S5_QS)
        cols = slice(q * LANE, (q + 1) * LANE)
        yq = _dot(hr_ref[:, hs].astype(BF16), cw_ref[q, :S5_QS, :]) + _dot(hi_ref[:, hs].astype(BF16), cw_ref[q, S5_QS:, :])
        if scale_add:
            y_ref[:, cols] = add[:, cols] * u[:, cols] + yq
        else:
            y_ref[:, cols] = add[:, cols] + yq

    fr_ref[...] = sr_ref[...]
    fi_ref[...] = si_ref[...]


def _s5_scan(u_tm, add, bw, cw, ar, ai, h0r, h0i, bsz, seq, tt, reverse, scale_add, l, dirn):
    rows = tt * bsz
    nt = seq // tt
    tmap = (lambda j: (nt - 1 - j, 0)) if reverse else (lambda j: (j, 0))
    wsel = lambda j: (l, dirn, 0, 0, 0)
    asel = lambda j: (l, dirn, 0, 0)
    add_spec = pl.BlockSpec((None, 1, S5_CH), lambda j: (l, 0, 0)) if scale_add else pl.BlockSpec((rows, S5_CH), tmap)
    return pl.pallas_call(
        functools.partial(_s5_kernel, bsz=bsz, tt=tt, reverse=reverse, scale_add=scale_add),
        grid=(nt,),
        in_specs=[
            pl.BlockSpec((rows, S5_CH), tmap),
            add_spec,
            pl.BlockSpec((None, None, S5_Q, LANE, 2 * S5_QS), wsel),
            pl.BlockSpec((None, None, S5_Q, 2 * S5_QS, LANE), wsel),
            pl.BlockSpec((None, None, 1, S5_NSTATE), asel),
            pl.BlockSpec((None, None, 1, S5_NSTATE), asel),
            pl.BlockSpec((bsz, S5_NSTATE), lambda j: (0, 0)),
            pl.BlockSpec((bsz, S5_NSTATE), lambda j: (0, 0)),
        ],
        out_specs=[
            pl.BlockSpec((rows, S5_CH), tmap),
            pl.BlockSpec((bsz, S5_NSTATE), lambda j: (0, 0)),
            pl.BlockSpec((bsz, S5_NSTATE), lambda j: (0, 0)),
        ],
        out_shape=[
            jax.ShapeDtypeStruct((seq * bsz, S5_CH), F32),
            jax.ShapeDtypeStruct((bsz, S5_NSTATE), F32),
            jax.ShapeDtypeStruct((bsz, S5_NSTATE), F32),
        ],
        scratch_shapes=[
            pltpu.VMEM((rows, S5_NSTATE), F32),
            pltpu.VMEM((rows, S5_NSTATE), F32),
            pltpu.VMEM((bsz, S5_NSTATE), F32),
            pltpu.VMEM((bsz, S5_NSTATE), F32),
        ],
        compiler_params=_cparams(("arbitrary",)),
        name="s5_scan",
    )(u_tm, add, bw, cw, ar, ai, h0r, h0i)


def _s5_weights(lam_re, lam_im, log_dt, b_re, b_im, c_re, c_im):
    dt = jnp.exp(log_dt)[..., None]
    mag = jnp.exp(lam_re * dt)
    ab_re = mag * jnp.cos(lam_im * dt)
    ab_im = mag * jnp.sin(lam_im * dt)
    den = lam_re * lam_re + lam_im * lam_im
    nr = ab_re - 1.0
    f_re = (nr * lam_re + ab_im * lam_im) / den
    f_im = (ab_im * lam_re - nr * lam_im) / den
    bb_re = f_re[..., None] * b_re - f_im[..., None] * b_im
    bb_im = f_re[..., None] * b_im + f_im[..., None] * b_re
    eye = jnp.eye(8, dtype=F32)
    lshape = lam_re.shape[:2]

    def pack_b(bb):
        t = bb.reshape(*lshape, S5_Q, 8, S5_STATE, S5_GROUP)
        return jnp.einsum("ldqrpc,rs->ldqrcsp", t, eye).reshape(*lshape, S5_Q, LANE, S5_QS)

    def pack_c(cc):
        t = cc.reshape(*lshape, S5_Q, 8, S5_GROUP, S5_STATE)
        return jnp.einsum("ldqrcp,rs->ldqrpsc", t, eye).reshape(*lshape, S5_Q, S5_QS, LANE)

    bw = jnp.concatenate([pack_b(bb_re), pack_b(bb_im)], axis=-1).astype(BF16)
    cw = jnp.concatenate([pack_c(c_re), -pack_c(c_im)], axis=-2).astype(BF16)
    ar = ab_re.reshape(*lshape, 1, S5_NSTATE)
    ai = ab_im.reshape(*lshape, 1, S5_NSTATE)
    return bw, cw, ar, ai


def _outproj_kernel(x_ref, mod_ref, om_ref, y_ref, og_ref, wglu_ref, bglu_ref, wout_ref, o_ref):
    y = y_ref[...]
    gate = _sigmoid(_dot(y.astype(BF16), wglu_ref[...]) + bglu_ref[...])
    gelu = 0.5 * y * (1.0 + jnp.tanh(math.sqrt(2.0 / math.pi) * (y + 0.044715 * (y * y * y))))
    s5 = (gelu * gate).astype(BF16)
    n_mla = MLA_HEADS * MLA_V
    mixed = (_dot(om_ref[...], wout_ref[0:n_mla, :])
             + _dot(s5, wout_ref[n_mla:n_mla + S5_CH, :])
             + _dot(og_ref[...], wout_ref[n_mla + S5_CH:, :]))
    o_ref[...] = x_ref[...] + mod_ref[5:6, :] * mixed


def _outproj(x, bsz, seq, mods_l, mod_base, per_batch_mod, o_mla, y_tm, o_gqa, w, l):
    n = bsz * seq
    tm = 256
    nt = seq // tm
    row = lambda b, j: (b * nt + j, 0)
    lsel = lambda b, j: (l, 0, 0)
    mod_map = (lambda b, j: (mod_base + b, 0, 0)) if per_batch_mod else (lambda b, j: (mod_base, 0, 0))
    return pl.pallas_call(
        _outproj_kernel,
        grid=(bsz, nt),
        in_specs=[
            pl.BlockSpec((tm, D_MODEL), row),
            pl.BlockSpec((None, N_MOD, D_MODEL), mod_map),
            pl.BlockSpec((tm, MLA_HEADS * MLA_V), row),
            pl.BlockSpec((tm, S5_CH), lambda b, j: (j, b)),
            pl.BlockSpec((tm, GQA_Q_HEADS * GQA_HEAD_DIM), row),
            pl.BlockSpec((None, S5_CH, S5_CH), lsel),
            pl.BlockSpec((None, 1, S5_CH), lsel),
            pl.BlockSpec((None, D_MODEL, D_MODEL), lsel),
        ],
        out_specs=pl.BlockSpec((tm, D_MODEL), row),
        out_shape=jax.ShapeDtypeStruct((n, D_MODEL), F32),
        compiler_params=_cparams(("parallel", "parallel")),
        name="outproj",
    )(x, mods_l, o_mla, y_tm, o_gqa, w["s5_w_glu"], w["s5_b_glu"], w["w_out"])


def _norm_kernel(x_ref, g_ref, o_ref):
    o_ref[...] = _rms(x_ref[...], g_ref[...])


def _final_norm(x, g):
    n = x.shape[0]
    tm = 1024
    return pl.pallas_call(
        _norm_kernel,
        grid=(n // tm,),
        in_specs=[pl.BlockSpec((tm, D_MODEL), lambda i: (i, 0)), pl.BlockSpec((1, D_MODEL), lambda i: (0, 0))],
        out_specs=pl.BlockSpec((tm, D_MODEL), lambda i: (i, 0)),
        out_shape=jax.ShapeDtypeStruct((n, D_MODEL), F32),
        compiler_params=_cparams(("parallel",)),
        name="final_norm",
    )(x, g)


def _rope_tables(seq, identity):
    if identity:
        one = jnp.ones((seq, LANE), F32)
        zero = jnp.zeros((seq, LANE), F32)
        return one, zero, zero, one, zero
    n_rows = seq // GRID_W
    row = jnp.broadcast_to(jnp.arange(n_rows, dtype=F32)[:, None], (n_rows, GRID_W)).reshape(seq)
    col = jnp.broadcast_to(jnp.arange(GRID_W, dtype=F32)[None, :], (n_rows, GRID_W)).reshape(seq)

    def angles(rot_dim):
        n_freq = rot_dim // 4
        inv = ROPE_THETA ** (-jnp.arange(n_freq, dtype=F32) / n_freq)
        ang = jnp.concatenate([row[:, None] * inv, col[:, None] * inv], axis=-1)
        return jnp.cos(ang), jnp.sin(ang)

    cm, sm = angles(MLA_ROPE)
    zero32 = jnp.zeros_like(sm)
    pad = jnp.zeros((seq, LANE - MLA_ROPE), F32)
    cos_m = jnp.concatenate([cm, cm, jnp.ones_like(pad)], axis=-1)
    sa = jnp.concatenate([zero32, sm, pad], axis=-1)
    sb = jnp.concatenate([-sm, zero32, pad], axis=-1)
    cg, sg = angles(GQA_HEAD_DIM)
    cos_g = jnp.concatenate([cg, cg], axis=-1)
    sin_g = jnp.concatenate([-sg, sg], axis=-1)
    return cos_m, sa, sb, cos_g, sin_g


def kernel(x_prompt, x_sample, cache_mla_ckv, cache_mla_krope, cache_gqa_k, cache_gqa_v, state_s5, c, c_ctx, w_ada, b_ada, norm_ffn1, ffn1_w_gate, ffn1_w_up, ffn1_w_down, norm_mix, w_in, mla_q_norm, mla_w_uq, mla_kv_norm, mla_w_ukv, s5_lambda_re, s5_lambda_im, s5_log_dt, s5_b_re, s5_b_im, s5_c_re, s5_c_im, s5_d, s5_w_glu, s5_b_glu, gqa_q_norm, gqa_k_norm, w_out, norm_ffn2, ffn2_w_gate, ffn2_w_up, ffn2_w_down, norm_final):
    L = DEPTH
    cb, cs, _ = x_prompt.shape
    lb, ls, _ = x_sample.shape
    past = cache_mla_ckv.shape[2]

    zpad = jnp.zeros((L, D_MODEL, LANE - MLA_ROPE), F32)
    w_in_p = jnp.concatenate([w_in[:, :, :MLA_Q_RANK + MLA_KV_RANK + MLA_ROPE], zpad,
                              w_in[:, :, MLA_Q_RANK + MLA_KV_RANK + MLA_ROPE:]], axis=-1).astype(BF16)
    wuq = mla_w_uq.reshape(L, MLA_Q_RANK, MLA_HEADS, MLA_NOPE + MLA_ROPE)
    wuq = jnp.pad(wuq, ((0, 0), (0, 0), (0, 0), (0, MLA_QK_PAD - MLA_NOPE - MLA_ROPE)))
    wuq = wuq.reshape(L, MLA_Q_RANK, MLA_HEADS * MLA_QK_PAD).astype(BF16)
    wukv = mla_w_ukv.reshape(L, MLA_KV_RANK, MLA_HEADS, MLA_NOPE + MLA_V)
    wukv = jnp.concatenate([wukv[..., :MLA_NOPE].reshape(L, MLA_KV_RANK, MLA_HEADS * MLA_NOPE),
                            wukv[..., MLA_NOPE:].reshape(L, MLA_KV_RANK, MLA_HEADS * MLA_V)], axis=-1).astype(BF16)
    w = {
        "norm_mix": norm_mix.reshape(L, 1, D_MODEL),
        "w_in": w_in_p,
        "mla_q_norm": mla_q_norm.reshape(L, 1, MLA_Q_RANK),
        "mla_w_uq": wuq,
        "mla_kv_norm": mla_kv_norm.reshape(L, 1, MLA_KV_RANK),
        "mla_w_ukv": wukv,
        "gqa_q_norm": gqa_q_norm.reshape(L, 1, GQA_HEAD_DIM),
        "gqa_k_norm": gqa_k_norm.reshape(L, 1, GQA_HEAD_DIM),
        "s5_w_glu": s5_w_glu.astype(BF16),
        "s5_b_glu": s5_b_glu.reshape(L, 1, S5_CH),
        "w_out": w_out.astype(BF16),
    }
    f1 = (norm_ffn1.reshape(L, 1, D_MODEL), ffn1_w_gate.astype(BF16), ffn1_w_up.astype(BF16), ffn1_w_down.astype(BF16))
    f2 = (norm_ffn2.reshape(L, 1, D_MODEL), ffn2_w_gate.astype(BF16), ffn2_w_up.astype(BF16), ffn2_w_down.astype(BF16))
    bw, cw, ar, ai = _s5_weights(s5_lambda_re, s5_lambda_im, s5_log_dt, s5_b_re, s5_b_im, s5_c_re, s5_c_im)
    d_skip = s5_d.reshape(L, 1, S5_CH)
    rope_ctx = _rope_tables(256, True)
    rope_lat = _rope_tables(ls, False)
    cache_kr_pad = jnp.pad(cache_mla_krope, ((0, 0), (0, 0), (0, 0), (0, LANE - MLA_ROPE)))
    cache_k = cache_gqa_k.reshape(lb, L, past, GQA_KV_HEADS * GQA_HEAD_DIM)
    cache_vt = jnp.swapaxes(cache_gqa_v.reshape(lb, L, past, GQA_KV_HEADS * GQA_HEAD_DIM), 2, 3).astype(BF16)
    h0 = state_s5.reshape(lb, L, 2, S5_NSTATE, 2)
    zero_state = jnp.zeros((cb, S5_NSTATE), F32)

    cmat = jnp.concatenate([c_ctx[None, :], c, jnp.zeros((MOD_ROWS - 1 - lb, D_MODEL), F32)], axis=0)
    mods = _modulation(cmat, w_ada, b_ada)

    xp = x_prompt.reshape(cb * cs, D_MODEL)
    xs = x_sample.reshape(lb * ls, D_MODEL)
    st_ckv, st_kr, st_k, st_v, st_s5 = [], [], [], [], []

    def mixer(x, bsz, seq, mods_l, mod_base, per_batch_mod, l, rope, is_ctx):
        outs = _inproj(x, bsz, seq, mods_l, mod_base, per_batch_mod, w, l, rope, is_ctx)
        qm, km, vm, u_tm, gq, gk, gv = outs[:7]
        mrep, grep = 1, GQA_Q_HEADS // GQA_KV_HEADS
        mla_parts = [_kv_part_2d(km, vm, seq, MLA_QK_PAD, MLA_V, MLA_HEADS, mrep)]
        gqa_parts = [_kv_part_2d(gk, gv, seq, GQA_HEAD_DIM, GQA_HEAD_DIM, GQA_KV_HEADS, grep)]
        if is_ctx:
            h0r = h0i = (zero_state, zero_state)
            tq, tt = 256, 32
        else:
            km_c, vm_c = _kvup(cache_mla_ckv, cache_kr_pad, w["mla_w_ukv"], l)
            mla_parts = [_kv_part_2d(km_c, vm_c, past, MLA_QK_PAD, MLA_V, MLA_HEADS, mrep)] + mla_parts
            gqa_parts = [_kv_part_cache(cache_k, cache_vt, past, GQA_HEAD_DIM, GQA_HEAD_DIM, grep, l)] + gqa_parts
            h0r = (h0[:, l, 0, :, 0], h0[:, l, 1, :, 0])
            h0i = (h0[:, l, 0, :, 1], h0[:, l, 1, :, 1])
            tq, tt = 1024, 64
        o_mla = _attention(qm, mla_parts, bsz, seq, MLA_HEADS, MLA_QK_PAD, MLA_V, tq)
        o_gqa = _attention(gq, gqa_parts, bsz, seq, GQA_Q_HEADS, GQA_HEAD_DIM, GQA_HEAD_DIM, tq)
        u2 = u_tm.reshape(seq * bsz, S5_CH)
        y_f, ffr, ffi = _s5_scan(u2, d_skip, bw, cw, ar, ai, h0r[0], h0i[0], bsz, seq, tt, False, True, l, 0)
        y, fbr, fbi = _s5_scan(u2, y_f, bw, cw, ar, ai, h0r[1], h0i[1], bsz, seq, tt, True, False, l, 1)
        x = _outproj(x, bsz, seq, mods_l, mod_base, per_batch_mod, o_mla, y.reshape(seq, bsz * S5_CH), o_gqa, w, l)
        extra = None
        if is_ctx:
            ckv, kr, kg, vg = outs[7:]
            s5s = jnp.stack([jnp.stack([ffr, ffi], axis=-1), jnp.stack([fbr, fbi], axis=-1)], axis=1)
            extra = (ckv.reshape(bsz, seq, MLA_KV_RANK), kr.reshape(bsz, seq, MLA_ROPE),
                     kg.reshape(bsz, seq, GQA_KV_HEADS, GQA_HEAD_DIM), vg.reshape(bsz, seq, GQA_KV_HEADS, GQA_HEAD_DIM),
                     s5s.reshape(bsz, 2, S5_GROUPS, S5_STATE, 2))
        return x, extra

    for l in range(L):
        mods_l = mods[l]
        xp = _ffn(xp, mods_l, 0, cb * cs, *f1, l, 0)
        xp, (ckv, kr, kg, vg, s5s) = mixer(xp, cb, cs, mods_l, 0, False, l, rope_ctx, True)
        xp = _ffn(xp, mods_l, 0, cb * cs, *f2, l, 6)
        st_ckv.append(ckv)
        st_kr.append(kr)
        st_k.append(kg)
        st_v.append(vg)
        st_s5.append(s5s)
        xs = _ffn(xs, mods_l, 1, ls, *f1, l, 0)
        xs, _ = mixer(xs, lb, ls, mods_l, 1, True, l, rope_lat, False)
        xs = _ffn(xs, mods_l, 1, ls, *f2, l, 6)

    g_final = norm_final.reshape(1, D_MODEL)
    y_prompt = _final_norm(xp, g_final).reshape(cb, cs, D_MODEL)
    y_sample = _final_norm(xs, g_final).reshape(lb, ls, D_MODEL)
    return (y_prompt, y_sample, jnp.stack(st_ckv, axis=1), jnp.stack(st_kr, axis=1), jnp.stack(st_k, axis=1),
            jnp.stack(st_v, axis=1), jnp.stack(st_s5, axis=1))
```

```python
import functools
import math

import jax
import jax.numpy as jnp
from jax import lax
from jax.experimental import pallas as pl
from jax.experimental.pallas import tpu as pltpu

D_MODEL = 2048
DEPTH = 4
GRID_W = 64
ROPE_THETA = 10000.0
EPS = 1e-6
N_MOD = 9
D_FF = 2 * D_MODEL

MLA_HEADS = 8
MLA_Q_RANK = D_MODEL // 4
MLA_KV_RANK = D_MODEL // 8
MLA_NOPE = 128
MLA_ROPE = 64
MLA_V = 128
MLA_QK_PAD = 256

S5_CH = D_MODEL // 4
S5_GROUP = 16
S5_GROUPS = S5_CH // S5_GROUP
S5_STATE = 64
S5_NSTATE = S5_GROUPS * S5_STATE

GQA_Q_HEADS = 4
GQA_KV_HEADS = 2
GQA_HEAD_DIM = 128

MLA_SCALE = 1.0 / math.sqrt(MLA_NOPE + MLA_ROPE)
GQA_SCALE = 1.0 / math.sqrt(GQA_HEAD_DIM)
LOG2E = math.log2(math.e)
MLA_QSCALE = MLA_SCALE * LOG2E
GQA_QSCALE = GQA_SCALE * LOG2E

ZQ0, ZKV0, ZKR0, ZU0, ZGQ0, ZGK0, ZGV0, Z_END = 0, 512, 768, 896, 1408, 1920, 2176, 2432

LANE = 128
SUBLANE = 8
MOD_ROWS = 16

F32 = jnp.float32
BF16 = jnp.bfloat16

VMEM_LIMIT = 56 * 1024 * 1024


def _cparams(sem):
    return pltpu.CompilerParams(dimension_semantics=sem, vmem_limit_bytes=VMEM_LIMIT)


def _rms(x, g):
    return x * lax.rsqrt(jnp.mean(x * x, axis=-1, keepdims=True) + EPS) * g


def _sigmoid(x):
    return 1.0 / (1.0 + jnp.exp(-x))


def _dot(a, b):
    return jnp.dot(a, b, preferred_element_type=F32)


def _mod_kernel(c_ref, w_ref, b_ref, o_ref):
    c = c_ref[...]
    a = (c * _sigmoid(c)).astype(BF16)
    o_ref[...] = _dot(a, w_ref[...].astype(BF16)) + b_ref[...]


def _modulation(cmat, w_ada, b_ada):
    tn = 1024
    nd = N_MOD * D_MODEL
    out = pl.pallas_call(
        _mod_kernel,
        grid=(DEPTH, nd // tn),
        in_specs=[
            pl.BlockSpec((MOD_ROWS, D_MODEL), lambda l, j: (0, 0)),
            pl.BlockSpec((None, D_MODEL, tn), lambda l, j: (l, 0, j)),
            pl.BlockSpec((None, 1, tn), lambda l, j: (l, 0, j)),
        ],
        out_specs=pl.BlockSpec((None, MOD_ROWS, tn), lambda l, j: (l, 0, j)),
        out_shape=jax.ShapeDtypeStruct((DEPTH, MOD_ROWS, nd), F32),
        compiler_params=_cparams(("parallel", "parallel")),
        name="modulation",
    )(cmat, w_ada, b_ada.reshape(DEPTH, 1, nd))
    return out.reshape(DEPTH, MOD_ROWS, N_MOD, D_MODEL)


def _ffn_kernel(x_ref, xn_ref, mod_ref, modn_ref, g_ref, wg_ref, wu_ref, wd_ref, *rest, k0, final):
    if final:
        gf_ref, o_ref, hm_ref = rest
    else:
        o_ref, hm_ref = rest
    i = pl.program_id(0)
    j = pl.program_id(1)
    slot = lax.rem(i, 2)

    def modulated(x, m_ref):
        return (_rms(x, g_ref[...]) * (1.0 + m_ref[k0 + 1:k0 + 2, :]) + m_ref[k0:k0 + 1, :]).astype(BF16)

    @pl.when(j == 0)
    def _():
        o_ref[...] = x_ref[...]

    @pl.when((i == 0) & (j == 0))
    def _():
        hm_ref[0] = modulated(x_ref[...], mod_ref)

    rs = xn_ref.shape[0]
    y = None
    for c0 in range(0, wg_ref.shape[1], FFN_SUB):
        h = hm_ref[slot]
        g = _dot(h, wg_ref[:, c0:c0 + FFN_SUB])
        u = _dot(h, wu_ref[:, c0:c0 + FFN_SUB])
        a = (g * _sigmoid(g)) * u
        d = _dot(a.astype(BF16), wd_ref[c0:c0 + FFN_SUB, :])
        y = d if y is None else y + d
        if c0 == 0:
            hm_ref[1 - slot, pl.ds(pl.multiple_of(j * rs, rs), rs), :] = modulated(xn_ref[...], modn_ref)
    o_ref[...] += (0.5 * mod_ref[k0 + 2:k0 + 3, :]) * y

    if final:
        @pl.when(j == pl.num_programs(1) - 1)
        def _():
            o_ref[...] = _rms(o_ref[...], gf_ref[...])


FFN_SUB = 512


def _ffn(x, mods_l, mod_base, rows_per_mod, norm_g, wg, wu, wd, l, k0, final_g=None):
    n = x.shape[0]
    tm, tf = 512, 1024
    nrt, nff = n // tm, D_FF // tf
    rs = tm // nff
    nxt = lambda i: jnp.minimum(i + 1, nrt - 1)
    in_specs = [
        pl.BlockSpec((tm, D_MODEL), lambda i, j: (i, 0)),
        pl.BlockSpec((rs, D_MODEL), lambda i, j: (nxt(i) * nff + j, 0)),
        pl.BlockSpec((None, N_MOD, D_MODEL), lambda i, j: (mod_base + (i * tm) // rows_per_mod, 0, 0)),
        pl.BlockSpec((None, N_MOD, D_MODEL), lambda i, j: (mod_base + (nxt(i) * tm) // rows_per_mod, 0, 0)),
        pl.BlockSpec((None, 1, D_MODEL), lambda i, j: (l, 0, 0)),
        pl.BlockSpec((None, D_MODEL, tf), lambda i, j: (l, 0, j)),
        pl.BlockSpec((None, D_MODEL, tf), lambda i, j: (l, 0, j)),
        pl.BlockSpec((None, tf, D_MODEL), lambda i, j: (l, j, 0)),
    ]
    args = [x, x, mods_l, mods_l, norm_g, wg, wu, wd]
    if final_g is not None:
        in_specs.append(pl.BlockSpec((1, D_MODEL), lambda i, j: (0, 0)))
        args.append(final_g)
    return pl.pallas_call(
        functools.partial(_ffn_kernel, k0=k0, final=final_g is not None),
        grid=(nrt, nff),
        in_specs=in_specs,
        out_specs=pl.BlockSpec((tm, D_MODEL), lambda i, j: (i, 0)),
        out_shape=jax.ShapeDtypeStruct((n, D_MODEL), F32),
        scratch_shapes=[pltpu.VMEM((2, tm, D_MODEL), BF16)],
        compiler_params=_cparams(("arbitrary", "arbitrary")),
        name="ffn",
    )(*args)


def _rope_mla(t, cos, sa, sb):
    return t * cos + pltpu.roll(t, 32, 1) * sa + pltpu.roll(t, 96, 1) * sb


def _rope_gqa(t, cos, ss):
    return t * cos + pltpu.roll(t, 64, 1) * ss


def _inproj_kernel(x_ref, mod_ref, g_ref, win_ref, qn_ref, wuq_ref, kvn_ref, wukv_ref, gqn_ref, gkn_ref,
                   cm_ref, sa_ref, sb_ref, cg_ref, sg_ref,
                   qm_ref, km_ref, vm_ref, u_ref, gq_ref, gk_ref, gv_ref, *cache_refs):
    x = x_ref[...]
    h = (_rms(x, g_ref[...]) * (1.0 + mod_ref[4:5, :]) + mod_ref[3:4, :]).astype(BF16)
    z = _dot(h, win_ref[...])
    cm, sa, sb = cm_ref[...], sa_ref[...], sb_ref[...]
    cg, sg = cg_ref[...], sg_ref[...]

    ckv = _rms(z[:, ZKV0:ZKR0], kvn_ref[...])
    kr = _rope_mla(z[:, ZKR0:ZU0], cm, sa, sb)
    kr_b = kr.astype(BF16)

    q = _dot(_rms(z[:, ZQ0:ZKV0], qn_ref[...]).astype(BF16), wuq_ref[...])
    kv = _dot(ckv.astype(BF16), wukv_ref[...])
    for hd in range(MLA_HEADS):
        c0 = hd * MLA_QK_PAD
        qm_ref[:, c0:c0 + LANE] = (q[:, c0:c0 + LANE] * MLA_QSCALE).astype(BF16)
        qm_ref[:, c0 + LANE:c0 + 2 * LANE] = (
            _rope_mla(q[:, c0 + LANE:c0 + 2 * LANE], cm, sa, sb) * MLA_QSCALE).astype(BF16)
        km_ref[:, c0:c0 + LANE] = kv[:, hd * LANE:(hd + 1) * LANE].astype(BF16)
        km_ref[:, c0 + LANE:c0 + 2 * LANE] = kr_b
    vm_ref[...] = kv[:, MLA_HEADS * MLA_NOPE:].T.astype(BF16)

    u_ref[...] = z[:, ZU0:ZGQ0]

    for hd in range(GQA_Q_HEADS):
        t = _rms(z[:, ZGQ0 + hd * LANE:ZGQ0 + (hd + 1) * LANE], gqn_ref[...])
        gq_ref[:, hd * LANE:(hd + 1) * LANE] = (_rope_gqa(t, cg, sg) * GQA_QSCALE).astype(BF16)
    kgs = []
    for hd in range(GQA_KV_HEADS):
        t = _rms(z[:, ZGK0 + hd * LANE:ZGK0 + (hd + 1) * LANE], gkn_ref[...])
        t = _rope_gqa(t, cg, sg)
        kgs.append(t)
        gk_ref[:, hd * LANE:(hd + 1) * LANE] = t.astype(BF16)
    gv = z[:, ZGV0:Z_END]
    gv_ref[...] = gv.T.astype(BF16)

    if cache_refs:
        ckv_o, kr_o, kg_o, vg_o = cache_refs
        ckv_o[...] = ckv
        kr_o[...] = kr[:, :MLA_ROPE]
        for hd in range(GQA_KV_HEADS):
            kg_o[:, hd * LANE:(hd + 1) * LANE] = kgs[hd]
        vg_o[...] = gv


def _inproj(x, bsz, seq, mods_l, mod_base, per_batch_mod, w, l, rope, emit_cache):
    n = bsz * seq
    tm = 256
    nt = seq // tm
    cm, sa, sb, cg, sg = rope
    rope_blocks = cm.shape[0] // tm
    rmap = (lambda b, j: (j, 0)) if rope_blocks > 1 else (lambda b, j: (0, 0))
    row = lambda b, j: (b * nt + j, 0)
    lsel = lambda b, j: (l, 0, 0)
    mod_map = (lambda b, j: (mod_base + b, 0, 0)) if per_batch_mod else (lambda b, j: (mod_base, 0, 0))
    out_shape = [
        jax.ShapeDtypeStruct((n, MLA_HEADS * MLA_QK_PAD), BF16),
        jax.ShapeDtypeStruct((n, MLA_HEADS * MLA_QK_PAD), BF16),
        jax.ShapeDtypeStruct((bsz * MLA_HEADS * MLA_V, seq), BF16),
        jax.ShapeDtypeStruct((seq, bsz * S5_CH), F32),
        jax.ShapeDtypeStruct((n, GQA_Q_HEADS * GQA_HEAD_DIM), BF16),
        jax.ShapeDtypeStruct((n, GQA_KV_HEADS * GQA_HEAD_DIM), BF16),
        jax.ShapeDtypeStruct((bsz * GQA_KV_HEADS * GQA_HEAD_DIM, seq), BF16),
    ]
    out_specs = [
        pl.BlockSpec((tm, MLA_HEADS * MLA_QK_PAD), row),
        pl.BlockSpec((tm, MLA_HEADS * MLA_QK_PAD), row),
        pl.BlockSpec((MLA_HEADS * MLA_V, tm), lambda b, j: (b, j)),
        pl.BlockSpec((tm, S5_CH), lambda b, j: (j, b)),
        pl.BlockSpec((tm, GQA_Q_HEADS * GQA_HEAD_DIM), row),
        pl.BlockSpec((tm, GQA_KV_HEADS * GQA_HEAD_DIM), row),
        pl.BlockSpec((GQA_KV_HEADS * GQA_HEAD_DIM, tm), lambda b, j: (b, j)),
    ]
    if emit_cache:
        out_shape += [
            jax.ShapeDtypeStruct((n, MLA_KV_RANK), F32),
            jax.ShapeDtypeStruct((n, MLA_ROPE), F32),
            jax.ShapeDtypeStruct((n, GQA_KV_HEADS * GQA_HEAD_DIM), F32),
            jax.ShapeDtypeStruct((n, GQA_KV_HEADS * GQA_HEAD_DIM), F32),
        ]
        out_specs += [
            pl.BlockSpec((tm, MLA_KV_RANK), row),
            pl.BlockSpec((tm, MLA_ROPE), row),
            pl.BlockSpec((tm, GQA_KV_HEADS * GQA_HEAD_DIM), row),
            pl.BlockSpec((tm, GQA_KV_HEADS * GQA_HEAD_DIM), row),
        ]
    return pl.pallas_call(
        _inproj_kernel,
        grid=(bsz, nt),
        in_specs=[
            pl.BlockSpec((tm, D_MODEL), row),
            pl.BlockSpec((None, N_MOD, D_MODEL), mod_map),
            pl.BlockSpec((None, 1, D_MODEL), lsel),
            pl.BlockSpec((None, D_MODEL, Z_END), lsel),
            pl.BlockSpec((None, 1, MLA_Q_RANK), lsel),
            pl.BlockSpec((None, MLA_Q_RANK, MLA_HEADS * MLA_QK_PAD), lsel),
            pl.BlockSpec((None, 1, MLA_KV_RANK), lsel),
            pl.BlockSpec((None, MLA_KV_RANK, MLA_HEADS * (MLA_NOPE + MLA_V)), lsel),
            pl.BlockSpec((None, 1, GQA_HEAD_DIM), lsel),
            pl.BlockSpec((None, 1, GQA_HEAD_DIM), lsel),
        ] + [pl.BlockSpec((tm, LANE), rmap)] * 5,
        out_specs=out_specs,
        out_shape=out_shape,
        compiler_params=_cparams(("parallel", "parallel")),
        name="inproj",
    )(x, mods_l, w["norm_mix"], w["w_in"], w["mla_q_norm"], w["mla_w_uq"], w["mla_kv_norm"], w["mla_w_ukv"],
      w["gqa_q_norm"], w["gqa_k_norm"], cm, sa, sb, cg, sg)


def _kvup_kernel(ckv_ref, kr_ref, wukv_ref, km_ref, vm_ref):
    kv = _dot(ckv_ref[...].astype(BF16), wukv_ref[...])
    kr_b = kr_ref[...].astype(BF16)
    for hd in range(MLA_HEADS):
        c0 = hd * MLA_QK_PAD
        km_ref[:, c0:c0 + LANE] = kv[:, hd * LANE:(hd + 1) * LANE].astype(BF16)
        km_ref[:, c0 + LANE:c0 + 2 * LANE] = kr_b
    vm_ref[...] = kv[:, MLA_HEADS * MLA_NOPE:].T.astype(BF16)


def _kvup(cache_ckv, cache_kr_pad, wukv, l):
    bsz, _, past, _ = cache_ckv.shape
    n = bsz * past
    return pl.pallas_call(
        _kvup_kernel,
        grid=(bsz,),
        in_specs=[
            pl.BlockSpec((None, None, past, MLA_KV_RANK), lambda b: (b, l, 0, 0)),
            pl.BlockSpec((None, None, past, LANE), lambda b: (b, l, 0, 0)),
            pl.BlockSpec((None, MLA_KV_RANK, MLA_HEADS * (MLA_NOPE + MLA_V)), lambda b: (l, 0, 0)),
        ],
        out_specs=[
            pl.BlockSpec((past, MLA_HEADS * MLA_QK_PAD), lambda b: (b, 0)),
            pl.BlockSpec((MLA_HEADS * MLA_V, past), lambda b: (b, 0)),
        ],
        out_shape=[
            jax.ShapeDtypeStruct((n, MLA_HEADS * MLA_QK_PAD), BF16),
            jax.ShapeDtypeStruct((bsz * MLA_HEADS * MLA_V, past), BF16),
        ],
        compiler_params=_cparams(("parallel",)),
        name="kvup",
    )(cache_ckv, cache_kr_pad, wukv)


ATTN_CK = 256
ATTN_SUB = 512


def _attn_kernel(*refs, nparts):
    q_ref = refs[0]
    k_refs = refs[1:1 + nparts]
    vt_refs = refs[1 + nparts:1 + 2 * nparts]
    o_ref = refs[1 + 2 * nparts]
    s_ref = refs[2 + 2 * nparts]
    tq = q_ref.shape[0]
    ts = min(tq, ATTN_SUB)
    dn = (((1,), (1,)), ((), ()))
    chunks = []
    off = 0
    for pi, k_ref in enumerate(k_refs):
        rows = k_ref.shape[0]
        for r0 in range(0, rows, ATTN_CK):
            ck = min(ATTN_CK, rows - r0)
            chunks.append((pi, r0, ck, off))
            off += ck

    maxes = []
    for q0 in range(0, tq, ts):
        q = q_ref[q0:q0 + ts, :]
        mpart = None
        for pi, r0, ck, off in chunks:
            s = lax.dot_general(k_refs[pi][r0:r0 + ck, :].astype(BF16), q, dn, preferred_element_type=F32)
            s_ref[off:off + ck, q0:q0 + ts] = s
            blk = jnp.max(s.reshape(ck // SUBLANE, SUBLANE, ts), axis=0)
            mpart = blk if mpart is None else jnp.maximum(mpart, blk)
        maxes.append(jnp.max(mpart, axis=0, keepdims=True))

    dv = vt_refs[0].shape[0]
    for q0, m in zip(range(0, tq, ts), maxes):
        lpart = jnp.zeros((SUBLANE, ts), F32)
        acc = jnp.zeros((dv, ts), F32)
        for pi, r0, ck, off in chunks:
            p = jnp.exp2(s_ref[off:off + ck, q0:q0 + ts] - m)
            lpart = lpart + jnp.sum(p.reshape(ck // SUBLANE, SUBLANE, ts), axis=0)
            acc = acc + _dot(vt_refs[pi][:, r0:r0 + ck].astype(BF16), p.astype(BF16))
        den = jnp.sum(lpart, axis=0, keepdims=True)
        o_ref[q0:q0 + ts, :] = (acc / den).T.astype(o_ref.dtype)


def _attention(q, parts, bsz, seq, heads, dk, dv, tq):
    nq = seq // tq
    n_keys = sum(p[4] for p in parts)
    assert all(p[4] % LANE == 0 for p in parts)
    in_specs = [pl.BlockSpec((tq, dk), lambda b, h, i: (b * nq + i, h))]
    in_specs += [p[2] for p in parts] + [p[3] for p in parts]
    args = [q] + [p[0] for p in parts] + [p[1] for p in parts]
    return pl.pallas_call(
        functools.partial(_attn_kernel, nparts=len(parts)),
        grid=(bsz, heads, nq),
        in_specs=in_specs,
        out_specs=pl.BlockSpec((tq, dv), lambda b, h, i: (b * nq + i, h)),
        out_shape=jax.ShapeDtypeStruct((bsz * seq, heads * dv), BF16),
        scratch_shapes=[pltpu.VMEM((n_keys, tq), F32)],
        compiler_params=_cparams(("parallel", "parallel", "arbitrary")),
        name="attention",
    )(*args)


def _kv_part_2d(k, vt, rows, dk, dv, kv_heads, rep):
    return (k, vt,
            pl.BlockSpec((rows, dk), lambda b, h, i: (b, h // rep)),
            pl.BlockSpec((dv, rows), lambda b, h, i: (b * kv_heads + h // rep, 0)), rows)


def _kv_part_cache(k, vt, rows, dk, dv, rep, l):
    return (k, vt,
            pl.BlockSpec((None, None, rows, dk), lambda b, h, i: (b, l, 0, h // rep)),
            pl.BlockSpec((None, None, dv, rows), lambda b, h, i: (b, l, h // rep, 0)), rows)


S5_LC = 512
S5_Q = S5_CH // LANE
S5_QS = S5_NSTATE // S5_Q


def _s5_kernel(uf_ref, ub_ref, d_ref, bw_ref, cw_ref, ar_ref, ai_ref, h0r_ref, h0i_ref,
               yf_ref, yb_ref, fr_ref, fi_ref, ut_ref, yt_ref, hr_ref, hi_ref, sr_ref, si_ref, *, bsz, tt):
    j = pl.program_id(0)

    @pl.when(j == 0)
    def _():
        sr_ref[...] = h0r_ref[...]
        si_ref[...] = h0i_ref[...]

    for d, (u_ref, y_ref) in enumerate(((uf_ref, yf_ref), (ub_ref, yb_ref))):
        for b in range(bsz):
            for q in range(S5_Q):
                c0 = b * S5_CH + q * LANE
                ut_ref[d, q, pl.ds(b, tt, stride=bsz), :] = u_ref[:, c0:c0 + LANE]
        for q in range(S5_Q):
            bu = _dot(ut_ref[d, q].astype(BF16), bw_ref[d, q])
            hr_ref[d, :, q * S5_QS:(q + 1) * S5_QS] = bu[:, :S5_QS]
            hi_ref[d, :, q * S5_QS:(q + 1) * S5_QS] = bu[:, S5_QS:]

        for c in range(S5_NSTATE // S5_LC):
            lanes = slice(c * S5_LC, (c + 1) * S5_LC)
            ar = jnp.broadcast_to(ar_ref[d, :, lanes], (bsz, S5_LC))
            ai = jnp.broadcast_to(ai_ref[d, :, lanes], (bsz, S5_LC))
            pr = sr_ref[d, :, lanes]
            pi = si_ref[d, :, lanes]
            for t in (range(tt - 1, -1, -1) if d == 1 else range(tt)):
                rows = slice(t * bsz, (t + 1) * bsz)
                nr = ar * pr - ai * pi + hr_ref[d, rows, lanes]
                ni = ar * pi + ai * pr + hi_ref[d, rows, lanes]
                hr_ref[d, rows, lanes] = nr
                hi_ref[d, rows, lanes] = ni
                pr, pi = nr, ni
            sr_ref[d, :, lanes] = pr
            si_ref[d, :, lanes] = pi

        for q in range(S5_Q):
            hs = slice(q * S5_QS, (q + 1) * S5_QS)
            cols = slice(q * LANE, (q + 1) * LANE)
            yq = (_dot(hr_ref[d, :, hs].astype(BF16), cw_ref[d, q, :S5_QS, :])
                  + _dot(hi_ref[d, :, hs].astype(BF16), cw_ref[d, q, S5_QS:, :]))
            if d == 0:
                yq = d_ref[:, cols] * ut_ref[d, q] + yq
            yt_ref[d, q] = yq
        for b in range(bsz):
            for q in range(S5_Q):
                c0 = b * S5_CH + q * LANE
                y_ref[:, c0:c0 + LANE] = yt_ref[d, q, pl.ds(b, tt, stride=bsz), :]

    fr_ref[...] = sr_ref[...]
    fi_ref[...] = si_ref[...]


def _s5_scan(u, d_skip, bw, cw, ar, ai, h0r, h0i, bsz, seq, tt, l):
    rows = tt * bsz
    nt = seq // tt
    fmap = lambda j: (j, 0)
    bmap = lambda j: (nt - 1 - j, 0)
    wsel = lambda j: (l, 0, 0, 0, 0)
    asel = lambda j: (l, 0, 0, 0)
    ssel = lambda j: (0, 0, 0)
    return pl.pallas_call(
        functools.partial(_s5_kernel, bsz=bsz, tt=tt),
        grid=(nt,),
        in_specs=[
            pl.BlockSpec((tt, bsz * S5_CH), fmap),
            pl.BlockSpec((tt, bsz * S5_CH), bmap),
            pl.BlockSpec((None, 1, S5_CH), lambda j: (l, 0, 0)),
            pl.BlockSpec((None, 2, S5_Q, LANE, 2 * S5_QS), wsel),
            pl.BlockSpec((None, 2, S5_Q, 2 * S5_QS, LANE), wsel),
            pl.BlockSpec((None, 2, 1, S5_NSTATE), asel),
            pl.BlockSpec((None, 2, 1, S5_NSTATE), asel),
            pl.BlockSpec((2, bsz, S5_NSTATE), ssel),
            pl.BlockSpec((2, bsz, S5_NSTATE), ssel),
        ],
        out_specs=[
            pl.BlockSpec((tt, bsz * S5_CH), fmap),
            pl.BlockSpec((tt, bsz * S5_CH), bmap),
            pl.BlockSpec((2, bsz, S5_NSTATE), ssel),
            pl.BlockSpec((2, bsz, S5_NSTATE), ssel),
        ],
        out_shape=[
            jax.ShapeDtypeStruct((seq, bsz * S5_CH), F32),
            jax.ShapeDtypeStruct((seq, bsz * S5_CH), F32),
            jax.ShapeDtypeStruct((2, bsz, S5_NSTATE), F32),
            jax.ShapeDtypeStruct((2, bsz, S5_NSTATE), F32),
        ],
        scratch_shapes=[
            pltpu.VMEM((2, S5_Q, rows, LANE), F32),
            pltpu.VMEM((2, S5_Q, rows, LANE), F32),
            pltpu.VMEM((2, rows, S5_NSTATE), F32),
            pltpu.VMEM((2, rows, S5_NSTATE), F32),
            pltpu.VMEM((2, bsz, S5_NSTATE), F32),
            pltpu.VMEM((2, bsz, S5_NSTATE), F32),
        ],
        compiler_params=_cparams(("arbitrary",)),
        name="s5_scan",
    )(u, u, d_skip, bw, cw, ar, ai, h0r, h0i)


def _s5_weights(lam_re, lam_im, log_dt, b_re, b_im, c_re, c_im):
    dt = jnp.exp(log_dt)[..., None]
    mag = jnp.exp(lam_re * dt)
    ab_re = mag * jnp.cos(lam_im * dt)
    ab_im = mag * jnp.sin(lam_im * dt)
    den = lam_re * lam_re + lam_im * lam_im
    nr = ab_re - 1.0
    f_re = (nr * lam_re + ab_im * lam_im) / den
    f_im = (ab_im * lam_re - nr * lam_im) / den
    bb_re = f_re[..., None] * b_re - f_im[..., None] * b_im
    bb_im = f_re[..., None] * b_im + f_im[..., None] * b_re
    eye = jnp.eye(8, dtype=F32)
    lshape = lam_re.shape[:2]

    def pack_b(bb):
        t = bb.reshape(*lshape, S5_Q, 8, S5_STATE, S5_GROUP)
        return jnp.einsum("ldqrpc,rs->ldqrcsp", t, eye).reshape(*lshape, S5_Q, LANE, S5_QS)

    def pack_c(cc):
        t = cc.reshape(*lshape, S5_Q, 8, S5_GROUP, S5_STATE)
        return jnp.einsum("ldqrcp,rs->ldqrpsc", t, eye).reshape(*lshape, S5_Q, S5_QS, LANE)

    bw = jnp.concatenate([pack_b(bb_re), pack_b(bb_im)], axis=-1).astype(BF16)
    cw = jnp.concatenate([pack_c(c_re), -pack_c(c_im)], axis=-2).astype(BF16)
    ar = ab_re.reshape(*lshape, 1, S5_NSTATE)
    ai = ab_im.reshape(*lshape, 1, S5_NSTATE)
    return bw, cw, ar, ai


def _outproj_kernel(x_ref, mod_ref, om_ref, yf_ref, yb_ref, og_ref, wglu_ref, bglu_ref, wout_ref, o_ref):
    y = yf_ref[...] + yb_ref[...]
    gate = _sigmoid(_dot(y.astype(BF16), wglu_ref[...]) + bglu_ref[...])
    gelu = 0.5 * y * (1.0 + jnp.tanh(math.sqrt(2.0 / math.pi) * (y + 0.044715 * (y * y * y))))
    s5 = (gelu * gate).astype(BF16)
    n_mla = MLA_HEADS * MLA_V
    mixed = (_dot(om_ref[...], wout_ref[0:n_mla, :])
             + _dot(s5, wout_ref[n_mla:n_mla + S5_CH, :])
             + _dot(og_ref[...], wout_ref[n_mla + S5_CH:, :]))
    o_ref[...] = x_ref[...] + mod_ref[5:6, :] * mixed


def _outproj(x, bsz, seq, mods_l, mod_base, per_batch_mod, o_mla, y_f, y_b, o_gqa, w, l):
    n = bsz * seq
    tm = 256
    nt = seq // tm
    row = lambda b, j: (b * nt + j, 0)
    lsel = lambda b, j: (l, 0, 0)
    mod_map = (lambda b, j: (mod_base + b, 0, 0)) if per_batch_mod else (lambda b, j: (mod_base, 0, 0))
    return pl.pallas_call(
        _outproj_kernel,
        grid=(bsz, nt),
        in_specs=[
            pl.BlockSpec((tm, D_MODEL), row),
            pl.BlockSpec((None, N_MOD, D_MODEL), mod_map),
            pl.BlockSpec((tm, MLA_HEADS * MLA_V), row),
            pl.BlockSpec((tm, S5_CH), lambda b, j: (j, b)),
            pl.BlockSpec((tm, S5_CH), lambda b, j: (j, b)),
            pl.BlockSpec((tm, GQA_Q_HEADS * GQA_HEAD_DIM), row),
            pl.BlockSpec((None, S5_CH, S5_CH), lsel),
            pl.BlockSpec((None, 1, S5_CH), lsel),
            pl.BlockSpec((None, D_MODEL, D_MODEL), lsel),
        ],
        out_specs=pl.BlockSpec((tm, D_MODEL), row),
        out_shape=jax.ShapeDtypeStruct((n, D_MODEL), F32),
        compiler_params=_cparams(("parallel", "parallel")),
        name="outproj",
    )(x, mods_l, o_mla, y_f, y_b, o_gqa, w["s5_w_glu"], w["s5_b_glu"], w["w_out"])


def _rope_tables(seq, identity):
    if identity:
        one = jnp.ones((seq, LANE), F32)
        zero = jnp.zeros((seq, LANE), F32)
        return one, zero, zero, one, zero
    n_rows = seq // GRID_W
    row = jnp.broadcast_to(jnp.arange(n_rows, dtype=F32)[:, None], (n_rows, GRID_W)).reshape(seq)
    col = jnp.broadcast_to(jnp.arange(GRID_W, dtype=F32)[None, :], (n_rows, GRID_W)).reshape(seq)

    def angles(rot_dim):
        n_freq = rot_dim // 4
        inv = ROPE_THETA ** (-jnp.arange(n_freq, dtype=F32) / n_freq)
        ang = jnp.concatenate([row[:, None] * inv, col[:, None] * inv], axis=-1)
        return jnp.cos(ang), jnp.sin(ang)

    cm, sm = angles(MLA_ROPE)
    zero32 = jnp.zeros_like(sm)
    pad = jnp.zeros((seq, LANE - MLA_ROPE), F32)
    cos_m = jnp.concatenate([cm, cm, jnp.ones_like(pad)], axis=-1)
    sa = jnp.concatenate([zero32, sm, pad], axis=-1)
    sb = jnp.concatenate([-sm, zero32, pad], axis=-1)
    cg, sg = angles(GQA_HEAD_DIM)
    cos_g = jnp.concatenate([cg, cg], axis=-1)
    sin_g = jnp.concatenate([-sg, sg], axis=-1)
    return cos_m, sa, sb, cos_g, sin_g


def kernel(x_prompt, x_sample, cache_mla_ckv, cache_mla_krope, cache_gqa_k, cache_gqa_v, state_s5, c, c_ctx, w_ada, b_ada, norm_ffn1, ffn1_w_gate, ffn1_w_up, ffn1_w_down, norm_mix, w_in, mla_q_norm, mla_w_uq, mla_kv_norm, mla_w_ukv, s5_lambda_re, s5_lambda_im, s5_log_dt, s5_b_re, s5_b_im, s5_c_re, s5_c_im, s5_d, s5_w_glu, s5_b_glu, gqa_q_norm, gqa_k_norm, w_out, norm_ffn2, ffn2_w_gate, ffn2_w_up, ffn2_w_down, norm_final):
    L = DEPTH
    cb, cs, _ = x_prompt.shape
    lb, ls, _ = x_sample.shape
    past = cache_mla_ckv.shape[2]

    zpad = jnp.zeros((L, D_MODEL, LANE - MLA_ROPE), F32)
    w_in_p = jnp.concatenate([w_in[:, :, :MLA_Q_RANK + MLA_KV_RANK + MLA_ROPE], zpad,
                              w_in[:, :, MLA_Q_RANK + MLA_KV_RANK + MLA_ROPE:]], axis=-1).astype(BF16)
    wuq = mla_w_uq.reshape(L, MLA_Q_RANK, MLA_HEADS, MLA_NOPE + MLA_ROPE)
    wuq = jnp.pad(wuq, ((0, 0), (0, 0), (0, 0), (0, MLA_QK_PAD - MLA_NOPE - MLA_ROPE)))
    wuq = wuq.reshape(L, MLA_Q_RANK, MLA_HEADS * MLA_QK_PAD).astype(BF16)
    wukv = mla_w_ukv.reshape(L, MLA_KV_RANK, MLA_HEADS, MLA_NOPE + MLA_V)
    wukv = jnp.concatenate([wukv[..., :MLA_NOPE].reshape(L, MLA_KV_RANK, MLA_HEADS * MLA_NOPE),
                            wukv[..., MLA_NOPE:].reshape(L, MLA_KV_RANK, MLA_HEADS * MLA_V)], axis=-1).astype(BF16)
    w = {
        "norm_mix": norm_mix.reshape(L, 1, D_MODEL),
        "w_in": w_in_p,
        "mla_q_norm": mla_q_norm.reshape(L, 1, MLA_Q_RANK),
        "mla_w_uq": wuq,
        "mla_kv_norm": mla_kv_norm.reshape(L, 1, MLA_KV_RANK),
        "mla_w_ukv": wukv,
        "gqa_q_norm": gqa_q_norm.reshape(L, 1, GQA_HEAD_DIM),
        "gqa_k_norm": gqa_k_norm.reshape(L, 1, GQA_HEAD_DIM),
        "s5_w_glu": s5_w_glu.astype(BF16),
        "s5_b_glu": s5_b_glu.reshape(L, 1, S5_CH),
        "w_out": w_out.astype(BF16),
    }
    f1 = (norm_ffn1.reshape(L, 1, D_MODEL), ffn1_w_gate.astype(BF16), ffn1_w_up.astype(BF16), ffn1_w_down.astype(BF16))
    f2 = (norm_ffn2.reshape(L, 1, D_MODEL), ffn2_w_gate.astype(BF16), ffn2_w_up.astype(BF16), ffn2_w_down.astype(BF16))
    bw, cw, ar, ai = _s5_weights(s5_lambda_re, s5_lambda_im, s5_log_dt, s5_b_re, s5_b_im, s5_c_re, s5_c_im)
    d_skip = s5_d.reshape(L, 1, S5_CH)
    rope_ctx = _rope_tables(256, True)
    rope_lat = _rope_tables(ls, False)
    cache_kr_pad = jnp.pad(cache_mla_krope, ((0, 0), (0, 0), (0, 0), (0, LANE - MLA_ROPE)))
    cache_k = cache_gqa_k.reshape(lb, L, past, GQA_KV_HEADS * GQA_HEAD_DIM)
    cache_vt = jnp.swapaxes(cache_gqa_v.reshape(lb, L, past, GQA_KV_HEADS * GQA_HEAD_DIM), 2, 3).astype(BF16)
    h0 = state_s5.reshape(lb, L, 2, S5_NSTATE, 2)
    zero_state = jnp.zeros((2, cb, S5_NSTATE), F32)

    cmat = jnp.concatenate([c_ctx[None, :], c, jnp.zeros((MOD_ROWS - 1 - lb, D_MODEL), F32)], axis=0)
    mods = _modulation(cmat, w_ada, b_ada)

    xp = x_prompt.reshape(cb * cs, D_MODEL)
    xs = x_sample.reshape(lb * ls, D_MODEL)
    st_ckv, st_kr, st_k, st_v, st_s5 = [], [], [], [], []

    def mixer(x, bsz, seq, mods_l, mod_base, per_batch_mod, l, rope, is_ctx):
        outs = _inproj(x, bsz, seq, mods_l, mod_base, per_batch_mod, w, l, rope, is_ctx)
        qm, km, vm, u_tm, gq, gk, gv = outs[:7]
        mrep, grep = 1, GQA_Q_HEADS // GQA_KV_HEADS
        mla_parts = [_kv_part_2d(km, vm, seq, MLA_QK_PAD, MLA_V, MLA_HEADS, mrep)]
        gqa_parts = [_kv_part_2d(gk, gv, seq, GQA_HEAD_DIM, GQA_HEAD_DIM, GQA_KV_HEADS, grep)]
        if is_ctx:
            h0r = h0i = zero_state
            tq, tt = 256, 32
        else:
            km_c, vm_c = _kvup(cache_mla_ckv, cache_kr_pad, w["mla_w_ukv"], l)
            mla_parts = [_kv_part_2d(km_c, vm_c, past, MLA_QK_PAD, MLA_V, MLA_HEADS, mrep)] + mla_parts
            gqa_parts = [_kv_part_cache(cache_k, cache_vt, past, GQA_HEAD_DIM, GQA_HEAD_DIM, grep, l)] + gqa_parts
            h0r = jnp.swapaxes(h0[:, l, :, :, 0], 0, 1)
            h0i = jnp.swapaxes(h0[:, l, :, :, 1], 0, 1)
            tq, tt = 1024, 64
        o_mla = _attention(qm, mla_parts, bsz, seq, MLA_HEADS, MLA_QK_PAD, MLA_V, tq)
        o_gqa = _attention(gq, gqa_parts, bsz, seq, GQA_Q_HEADS, GQA_HEAD_DIM, GQA_HEAD_DIM, tq)
        y_f, y_b, fin_r, fin_i = _s5_scan(u_tm, d_skip, bw, cw, ar, ai, h0r, h0i, bsz, seq, tt, l)
        x = _outproj(x, bsz, seq, mods_l, mod_base, per_batch_mod, o_mla, y_f, y_b, o_gqa, w, l)
        extra = None
        if is_ctx:
            ckv, kr, kg, vg = outs[7:]
            s5s = jnp.swapaxes(jnp.stack([fin_r, fin_i], axis=-1), 0, 1)
            extra = (ckv.reshape(bsz, seq, MLA_KV_RANK), kr.reshape(bsz, seq, MLA_ROPE),
                     kg.reshape(bsz, seq, GQA_KV_HEADS, GQA_HEAD_DIM), vg.reshape(bsz, seq, GQA_KV_HEADS, GQA_HEAD_DIM),
                     s5s.reshape(bsz, 2, S5_GROUPS, S5_STATE, 2))
        return x, extra

    g_final = norm_final.reshape(1, D_MODEL)
    for l in range(L):
        mods_l = mods[l]
        fin = g_final if l == L - 1 else None
        xp = _ffn(xp, mods_l, 0, cb * cs, *f1, l, 0)
        xp, (ckv, kr, kg, vg, s5s) = mixer(xp, cb, cs, mods_l, 0, False, l, rope_ctx, True)
        xp = _ffn(xp, mods_l, 0, cb * cs, *f2, l, 6, fin)
        st_ckv.append(ckv)
        st_kr.append(kr)
        st_k.append(kg)
        st_v.append(vg)
        st_s5.append(s5s)
        xs = _ffn(xs, mods_l, 1, ls, *f1, l, 0)
        xs, _ = mixer(xs, lb, ls, mods_l, 1, True, l, rope_lat, False)
        xs = _ffn(xs, mods_l, 1, ls, *f2, l, 6, fin)

    y_prompt = xp.reshape(cb, cs, D_MODEL)
    y_sample = xs.reshape(lb, ls, D_MODEL)
    return (y_prompt, y_sample, jnp.stack(st_ckv, axis=1), jnp.stack(st_kr, axis=1), jnp.stack(st_k, axis=1),
            jnp.stack(st_v, axis=1), jnp.stack(st_s5, axis=1))
```

```python
import functools
import math

import jax
import jax.numpy as jnp
from jax import lax
from jax.experimental import pallas as pl
from jax.experimental.pallas import tpu as pltpu

D_MODEL = 2048
DEPTH = 4
GRID_W = 64
ROPE_THETA = 10000.0
EPS = 1e-6
N_MOD = 9
D_FF = 2 * D_MODEL

MLA_HEADS = 8
MLA_Q_RANK = D_MODEL // 4
MLA_KV_RANK = D_MODEL // 8
MLA_NOPE = 128
MLA_ROPE = 64
MLA_V = 128
MLA_QK_PAD = 256

S5_CH = D_MODEL // 4
S5_GROUP = 16
S5_GROUPS = S5_CH // S5_GROUP
S5_STATE = 64
S5_NSTATE = S5_GROUPS * S5_STATE

GQA_Q_HEADS = 4
GQA_KV_HEADS = 2
GQA_HEAD_DIM = 128

MLA_SCALE = 1.0 / math.sqrt(MLA_NOPE + MLA_ROPE)
GQA_SCALE = 1.0 / math.sqrt(GQA_HEAD_DIM)
LOG2E = math.log2(math.e)
MLA_QSCALE = MLA_SCALE * LOG2E
GQA_QSCALE = GQA_SCALE * LOG2E

ZQ0, ZKV0, ZKR0, ZU0, ZGQ0, ZGK0, ZGV0, Z_END = 0, 512, 768, 896, 1408, 1920, 2176, 2432

LANE = 128
SUBLANE = 8
MOD_ROWS = 16

F32 = jnp.float32
BF16 = jnp.bfloat16

VMEM_LIMIT = 56 * 1024 * 1024
INPROJ_TM = 256
OUTPROJ_TM = 512
RESIDENT = pl.Buffered(1)


def _cparams(sem):
    return pltpu.CompilerParams(dimension_semantics=sem, vmem_limit_bytes=VMEM_LIMIT)


def _rms(x, g):
    return x * lax.rsqrt(jnp.mean(x * x, axis=-1, keepdims=True) + EPS) * g


def _sigmoid(x):
    return 1.0 / (1.0 + jnp.exp(-x))


def _dot(a, b):
    return jnp.dot(a, b, preferred_element_type=F32)


def _mod_kernel(c_ref, w_ref, b_ref, o_ref):
    c = c_ref[...]
    a = (c * _sigmoid(c)).astype(BF16)
    o_ref[...] = _dot(a, w_ref[...].astype(BF16)) + b_ref[...]


def _modulation(cmat, w_ada, b_ada):
    tn = 1024
    nd = N_MOD * D_MODEL
    out = pl.pallas_call(
        _mod_kernel,
        grid=(DEPTH, nd // tn),
        in_specs=[
            pl.BlockSpec((MOD_ROWS, D_MODEL), lambda l, j: (0, 0)),
            pl.BlockSpec((None, D_MODEL, tn), lambda l, j: (l, 0, j)),
            pl.BlockSpec((None, 1, tn), lambda l, j: (l, 0, j)),
        ],
        out_specs=pl.BlockSpec((None, MOD_ROWS, tn), lambda l, j: (l, 0, j)),
        out_shape=jax.ShapeDtypeStruct((DEPTH, MOD_ROWS, nd), F32),
        compiler_params=_cparams(("parallel", "parallel")),
        name="modulation",
    )(cmat, w_ada, b_ada.reshape(DEPTH, 1, nd))
    return out.reshape(DEPTH, MOD_ROWS, N_MOD, D_MODEL)


def _ffn_kernel(x_ref, xn_ref, mod_ref, modn_ref, g_ref, wg_ref, wu_ref, wd_ref, *rest, k0, final):
    if final:
        gf_ref, o_ref, hm_ref = rest
    else:
        o_ref, hm_ref = rest
    i = pl.program_id(0)
    j = pl.program_id(1)
    slot = lax.rem(i, 2)

    def modulated(x, m_ref):
        return (_rms(x, g_ref[...]) * (1.0 + m_ref[k0 + 1:k0 + 2, :]) + m_ref[k0:k0 + 1, :]).astype(BF16)

    @pl.when(j == 0)
    def _():
        o_ref[...] = x_ref[...]

    @pl.when((i == 0) & (j == 0))
    def _():
        hm_ref[0] = modulated(x_ref[...], mod_ref)

    rs = xn_ref.shape[0]
    y = None
    for c0 in range(0, wg_ref.shape[1], FFN_SUB):
        h = hm_ref[slot]
        g = _dot(h, wg_ref[:, c0:c0 + FFN_SUB])
        u = _dot(h, wu_ref[:, c0:c0 + FFN_SUB])
        a = (g * _sigmoid(g)) * u
        d = _dot(a.astype(BF16), wd_ref[c0:c0 + FFN_SUB, :])
        y = d if y is None else y + d
        if c0 == 0:
            hm_ref[1 - slot, pl.ds(pl.multiple_of(j * rs, rs), rs), :] = modulated(xn_ref[...], modn_ref)
    o_ref[...] += (0.5 * mod_ref[k0 + 2:k0 + 3, :]) * y

    if final:
        @pl.when(j == pl.num_programs(1) - 1)
        def _():
            o_ref[...] = _rms(o_ref[...], gf_ref[...])


FFN_SUB = 512


def _ffn(x, mods_l, mod_base, rows_per_mod, norm_g, wg, wu, wd, l, k0, final_g=None):
    n = x.shape[0]
    tm, tf = 512, 1024
    nrt, nff = n // tm, D_FF // tf
    rs = tm // nff
    nxt = lambda i: jnp.minimum(i + 1, nrt - 1)
    in_specs = [
        pl.BlockSpec((tm, D_MODEL), lambda i, j: (i, 0)),
        pl.BlockSpec((rs, D_MODEL), lambda i, j: (nxt(i) * nff + j, 0)),
        pl.BlockSpec((None, N_MOD, D_MODEL), lambda i, j: (mod_base + (i * tm) // rows_per_mod, 0, 0)),
        pl.BlockSpec((None, N_MOD, D_MODEL), lambda i, j: (mod_base + (nxt(i) * tm) // rows_per_mod, 0, 0)),
        pl.BlockSpec((None, 1, D_MODEL), lambda i, j: (l, 0, 0)),
        pl.BlockSpec((None, D_MODEL, tf), lambda i, j: (l, 0, j)),
        pl.BlockSpec((None, D_MODEL, tf), lambda i, j: (l, 0, j)),
        pl.BlockSpec((None, tf, D_MODEL), lambda i, j: (l, j, 0)),
    ]
    args = [x, x, mods_l, mods_l, norm_g, wg, wu, wd]
    if final_g is not None:
        in_specs.append(pl.BlockSpec((1, D_MODEL), lambda i, j: (0, 0)))
        args.append(final_g)
    return pl.pallas_call(
        functools.partial(_ffn_kernel, k0=k0, final=final_g is not None),
        grid=(nrt, nff),
        in_specs=in_specs,
        out_specs=pl.BlockSpec((tm, D_MODEL), lambda i, j: (i, 0)),
        out_shape=jax.ShapeDtypeStruct((n, D_MODEL), F32),
        scratch_shapes=[pltpu.VMEM((2, tm, D_MODEL), BF16)],
        compiler_params=_cparams(("arbitrary", "arbitrary")),
        name="ffn",
    )(*args)


def _rope_mla(t, cos, sa, sb):
    return t * cos + pltpu.roll(t, 32, 1) * sa + pltpu.roll(t, 96, 1) * sb


def _rope_gqa(t, cos, ss):
    return t * cos + pltpu.roll(t, 64, 1) * ss


def _inproj_kernel(x_ref, mod_ref, g_ref, win_ref, qn_ref, wuq_ref, kvn_ref, wukv_ref, gqn_ref, gkn_ref,
                   cm_ref, sa_ref, sb_ref, cg_ref, sg_ref,
                   qm_ref, km_ref, vm_ref, u_ref, gq_ref, gk_ref, gv_ref, *cache_refs):
    x = x_ref[...]
    h = (_rms(x, g_ref[...]) * (1.0 + mod_ref[4:5, :]) + mod_ref[3:4, :]).astype(BF16)
    z = _dot(h, win_ref[...])
    cm, sa, sb = cm_ref[...], sa_ref[...], sb_ref[...]
    cg, sg = cg_ref[...], sg_ref[...]

    ckv = _rms(z[:, ZKV0:ZKR0], kvn_ref[...])
    kr = _rope_mla(z[:, ZKR0:ZU0], cm, sa, sb)
    kr_b = kr.astype(BF16)

    q = _dot(_rms(z[:, ZQ0:ZKV0], qn_ref[...]).astype(BF16), wuq_ref[...])
    kv = _dot(ckv.astype(BF16), wukv_ref[...])
    for hd in range(MLA_HEADS):
        c0 = hd * MLA_QK_PAD
        qm_ref[:, c0:c0 + LANE] = (q[:, c0:c0 + LANE] * MLA_QSCALE).astype(BF16)
        qm_ref[:, c0 + LANE:c0 + 2 * LANE] = (
            _rope_mla(q[:, c0 + LANE:c0 + 2 * LANE], cm, sa, sb) * MLA_QSCALE).astype(BF16)
        km_ref[:, c0:c0 + LANE] = kv[:, hd * LANE:(hd + 1) * LANE].astype(BF16)
        km_ref[:, c0 + LANE:c0 + 2 * LANE] = kr_b
    vm_ref[...] = kv[:, MLA_HEADS * MLA_NOPE:].T.astype(BF16)

    u_ref[...] = z[:, ZU0:ZGQ0]

    for hd in range(GQA_Q_HEADS):
        t = _rms(z[:, ZGQ0 + hd * LANE:ZGQ0 + (hd + 1) * LANE], gqn_ref[...])
        gq_ref[:, hd * LANE:(hd + 1) * LANE] = (_rope_gqa(t, cg, sg) * GQA_QSCALE).astype(BF16)
    kgs = []
    for hd in range(GQA_KV_HEADS):
        t = _rms(z[:, ZGK0 + hd * LANE:ZGK0 + (hd + 1) * LANE], gkn_ref[...])
        t = _rope_gqa(t, cg, sg)
        kgs.append(t)
        gk_ref[:, hd * LANE:(hd + 1) * LANE] = t.astype(BF16)
    gv = z[:, ZGV0:Z_END]
    gv_ref[...] = gv.T.astype(BF16)

    if cache_refs:
        ckv_o, kr_o, kg_o, vg_o = cache_refs
        ckv_o[...] = ckv
        kr_o[...] = kr[:, :MLA_ROPE]
        for hd in range(GQA_KV_HEADS):
            kg_o[:, hd * LANE:(hd + 1) * LANE] = kgs[hd]
        vg_o[...] = gv


def _inproj(x, bsz, seq, mods_l, mod_base, per_batch_mod, w, l, rope, emit_cache):
    n = bsz * seq
    tm = min(INPROJ_TM, seq)
    nt = seq // tm
    cm, sa, sb, cg, sg = rope
    rope_blocks = cm.shape[0] // tm
    rmap = (lambda b, j: (j, 0)) if rope_blocks > 1 else (lambda b, j: (0, 0))
    row = lambda b, j: (b * nt + j, 0)
    lsel = lambda b, j: (l, 0, 0)
    mod_map = (lambda b, j: (mod_base + b, 0, 0)) if per_batch_mod else (lambda b, j: (mod_base, 0, 0))
    out_shape = [
        jax.ShapeDtypeStruct((n, MLA_HEADS * MLA_QK_PAD), BF16),
        jax.ShapeDtypeStruct((n, MLA_HEADS * MLA_QK_PAD), BF16),
        jax.ShapeDtypeStruct((bsz * MLA_HEADS * MLA_V, seq), BF16),
        jax.ShapeDtypeStruct((seq, bsz * S5_CH), F32),
        jax.ShapeDtypeStruct((n, GQA_Q_HEADS * GQA_HEAD_DIM), BF16),
        jax.ShapeDtypeStruct((n, GQA_KV_HEADS * GQA_HEAD_DIM), BF16),
        jax.ShapeDtypeStruct((bsz * GQA_KV_HEADS * GQA_HEAD_DIM, seq), BF16),
    ]
    out_specs = [
        pl.BlockSpec((tm, MLA_HEADS * MLA_QK_PAD), row),
        pl.BlockSpec((tm, MLA_HEADS * MLA_QK_PAD), row),
        pl.BlockSpec((MLA_HEADS * MLA_V, tm), lambda b, j: (b, j)),
        pl.BlockSpec((tm, S5_CH), lambda b, j: (j, b)),
        pl.BlockSpec((tm, GQA_Q_HEADS * GQA_HEAD_DIM), row),
        pl.BlockSpec((tm, GQA_KV_HEADS * GQA_HEAD_DIM), row),
        pl.BlockSpec((GQA_KV_HEADS * GQA_HEAD_DIM, tm), lambda b, j: (b, j)),
    ]
    if emit_cache:
        out_shape += [
            jax.ShapeDtypeStruct((n, MLA_KV_RANK), F32),
            jax.ShapeDtypeStruct((n, MLA_ROPE), F32),
            jax.ShapeDtypeStruct((n, GQA_KV_HEADS * GQA_HEAD_DIM), F32),
            jax.ShapeDtypeStruct((n, GQA_KV_HEADS * GQA_HEAD_DIM), F32),
        ]
        out_specs += [
            pl.BlockSpec((tm, MLA_KV_RANK), row),
            pl.BlockSpec((tm, MLA_ROPE), row),
            pl.BlockSpec((tm, GQA_KV_HEADS * GQA_HEAD_DIM), row),
            pl.BlockSpec((tm, GQA_KV_HEADS * GQA_HEAD_DIM), row),
        ]
    return pl.pallas_call(
        _inproj_kernel,
        grid=(bsz, nt),
        in_specs=[
            pl.BlockSpec((tm, D_MODEL), row),
            pl.BlockSpec((None, N_MOD, D_MODEL), mod_map),
            pl.BlockSpec((None, 1, D_MODEL), lsel),
            pl.BlockSpec((None, D_MODEL, Z_END), lsel, pipeline_mode=RESIDENT),
            pl.BlockSpec((None, 1, MLA_Q_RANK), lsel),
            pl.BlockSpec((None, MLA_Q_RANK, MLA_HEADS * MLA_QK_PAD), lsel, pipeline_mode=RESIDENT),
            pl.BlockSpec((None, 1, MLA_KV_RANK), lsel),
            pl.BlockSpec((None, MLA_KV_RANK, MLA_HEADS * (MLA_NOPE + MLA_V)), lsel, pipeline_mode=RESIDENT),
            pl.BlockSpec((None, 1, GQA_HEAD_DIM), lsel),
            pl.BlockSpec((None, 1, GQA_HEAD_DIM), lsel),
        ] + [pl.BlockSpec((tm, LANE), rmap)] * 5,
        out_specs=out_specs,
        out_shape=out_shape,
        compiler_params=_cparams(("parallel", "parallel")),
        name="inproj",
    )(x, mods_l, w["norm_mix"], w["w_in"], w["mla_q_norm"], w["mla_w_uq"], w["mla_kv_norm"], w["mla_w_ukv"],
      w["gqa_q_norm"], w["gqa_k_norm"], cm, sa, sb, cg, sg)


def _kvup_kernel(ckv_ref, kr_ref, wukv_ref, km_ref, vm_ref):
    kv = _dot(ckv_ref[...].astype(BF16), wukv_ref[...])
    kr_b = kr_ref[...].astype(BF16)
    for hd in range(MLA_HEADS):
        c0 = hd * MLA_QK_PAD
        km_ref[:, c0:c0 + LANE] = kv[:, hd * LANE:(hd + 1) * LANE].astype(BF16)
        km_ref[:, c0 + LANE:c0 + 2 * LANE] = kr_b
    vm_ref[...] = kv[:, MLA_HEADS * MLA_NOPE:].T.astype(BF16)


def _kvup(cache_ckv, cache_kr_pad, wukv, l):
    bsz, _, past, _ = cache_ckv.shape
    n = bsz * past
    return pl.pallas_call(
        _kvup_kernel,
        grid=(bsz,),
        in_specs=[
            pl.BlockSpec((None, None, past, MLA_KV_RANK), lambda b: (b, l, 0, 0)),
            pl.BlockSpec((None, None, past, LANE), lambda b: (b, l, 0, 0)),
            pl.BlockSpec((None, MLA_KV_RANK, MLA_HEADS * (MLA_NOPE + MLA_V)), lambda b: (l, 0, 0)),
        ],
        out_specs=[
            pl.BlockSpec((past, MLA_HEADS * MLA_QK_PAD), lambda b: (b, 0)),
            pl.BlockSpec((MLA_HEADS * MLA_V, past), lambda b: (b, 0)),
        ],
        out_shape=[
            jax.ShapeDtypeStruct((n, MLA_HEADS * MLA_QK_PAD), BF16),
            jax.ShapeDtypeStruct((bsz * MLA_HEADS * MLA_V, past), BF16),
        ],
        compiler_params=_cparams(("parallel",)),
        name="kvup",
    )(cache_ckv, cache_kr_pad, wukv)


ATTN_CK = 256
ATTN_SUB = 512
ATTN_LAT_HPS = 2


def _attn_kernel(*refs, nparts, hps, rep, dk, dv):
    q_ref = refs[0]
    k_refs = refs[1:1 + nparts]
    vt_refs = refs[1 + nparts:1 + 2 * nparts]
    o_ref = refs[1 + 2 * nparts]
    s_ref = refs[2 + 2 * nparts]
    tq = q_ref.shape[0]
    ts = min(tq, ATTN_SUB)
    dn = (((1,), (1,)), ((), ()))
    chunks = []
    off = 0
    for pi, k_ref in enumerate(k_refs):
        rows = k_ref.shape[0]
        for r0 in range(0, rows, ATTN_CK):
            ck = min(ATTN_CK, rows - r0)
            chunks.append((pi, r0, ck, off))
            off += ck

    items = [(hh, q0) for hh in range(hps) for q0 in range(0, tq, ts)]

    maxes = []
    for hh, q0 in items:
        kv = hh // rep
        q = q_ref[q0:q0 + ts, hh * dk:(hh + 1) * dk]
        mpart = None
        for pi, r0, ck, off in chunks:
            k = k_refs[pi][r0:r0 + ck, kv * dk:(kv + 1) * dk].astype(BF16)
            s = lax.dot_general(k, q, dn, preferred_element_type=F32)
            s_ref[hh, off:off + ck, q0:q0 + ts] = s
            blk = jnp.max(s.reshape(ck // SUBLANE, SUBLANE, ts), axis=0)
            mpart = blk if mpart is None else jnp.maximum(mpart, blk)
        maxes.append(jnp.max(mpart, axis=0, keepdims=True))

    for (hh, q0), m in zip(items, maxes):
        kv = hh // rep
        lpart = jnp.zeros((SUBLANE, ts), F32)
        acc = jnp.zeros((dv, ts), F32)
        for pi, r0, ck, off in chunks:
            p = jnp.exp2(s_ref[hh, off:off + ck, q0:q0 + ts] - m)
            lpart = lpart + jnp.sum(p.reshape(ck // SUBLANE, SUBLANE, ts), axis=0)
            vt = vt_refs[pi][kv * dv:(kv + 1) * dv, r0:r0 + ck].astype(BF16)
            acc = acc + _dot(vt, p.astype(BF16))
        den = jnp.sum(lpart, axis=0, keepdims=True)
        o_ref[q0:q0 + ts, hh * dv:(hh + 1) * dv] = (acc / den).T.astype(o_ref.dtype)


def _attention(q, parts, bsz, seq, heads, rep, dk, dv, tq, hps):
    nq = seq // tq
    n_keys = sum(p[4] for p in parts)
    assert all(p[4] % LANE == 0 for p in parts) and heads % hps == 0 and hps % rep == 0
    in_specs = [pl.BlockSpec((tq, hps * dk), lambda b, h, i: (b * nq + i, h))]
    in_specs += [p[2] for p in parts] + [p[3] for p in parts]
    args = [q] + [p[0] for p in parts] + [p[1] for p in parts]
    return pl.pallas_call(
        functools.partial(_attn_kernel, nparts=len(parts), hps=hps, rep=rep, dk=dk, dv=dv),
        grid=(bsz, heads // hps, nq),
        in_specs=in_specs,
        out_specs=pl.BlockSpec((tq, hps * dv), lambda b, h, i: (b * nq + i, h)),
        out_shape=jax.ShapeDtypeStruct((bsz * seq, heads * dv), BF16),
        scratch_shapes=[pltpu.VMEM((hps, n_keys, tq), F32)],
        compiler_params=_cparams(("parallel", "parallel", "arbitrary")),
        name="attention",
    )(*args)


def _kv_part_2d(k, vt, rows, dk, dv, kv_heads, kps):
    ng = kv_heads // kps
    return (k, vt,
            pl.BlockSpec((rows, kps * dk), lambda b, h, i: (b, h)),
            pl.BlockSpec((kps * dv, rows), lambda b, h, i: (b * ng + h, 0)), rows)


def _kv_part_cache(k, vt, rows, dk, dv, kps, l):
    return (k, vt,
            pl.BlockSpec((None, None, rows, kps * dk), lambda b, h, i: (b, l, 0, h)),
            pl.BlockSpec((None, None, kps * dv, rows), lambda b, h, i: (b, l, h, 0)), rows)


S5_LC = 512
S5_Q = S5_CH // LANE
S5_QS = S5_NSTATE // S5_Q


def _s5_kernel(uf_ref, ub_ref, d_ref, bw_ref, cw_ref, ar_ref, ai_ref, h0r_ref, h0i_ref,
               yf_ref, yb_ref, fr_ref, fi_ref, ut_ref, yt_ref, hr_ref, hi_ref, sr_ref, si_ref, *, bsz, tt):
    j = pl.program_id(0)

    @pl.when(j == 0)
    def _():
        sr_ref[...] = h0r_ref[...]
        si_ref[...] = h0i_ref[...]

    for d, (u_ref, y_ref) in enumerate(((uf_ref, yf_ref), (ub_ref, yb_ref))):
        for b in range(bsz):
            for q in range(S5_Q):
                c0 = b * S5_CH + q * LANE
                ut_ref[d, q, pl.ds(b, tt, stride=bsz), :] = u_ref[:, c0:c0 + LANE]
        for q in range(S5_Q):
            bu = _dot(ut_ref[d, q].astype(BF16), bw_ref[d, q])
            hr_ref[d, :, q * S5_QS:(q + 1) * S5_QS] = bu[:, :S5_QS]
            hi_ref[d, :, q * S5_QS:(q + 1) * S5_QS] = bu[:, S5_QS:]

        for c in range(S5_NSTATE // S5_LC):
            lanes = slice(c * S5_LC, (c + 1) * S5_LC)
            ar = jnp.broadcast_to(ar_ref[d, :, lanes], (bsz, S5_LC))
            ai = jnp.broadcast_to(ai_ref[d, :, lanes], (bsz, S5_LC))
            pr = sr_ref[d, :, lanes]
            pi = si_ref[d, :, lanes]
            for t in (range(tt - 1, -1, -1) if d == 1 else range(tt)):
                rows = slice(t * bsz, (t + 1) * bsz)
                nr = ar * pr - ai * pi + hr_ref[d, rows, lanes]
                ni = ar * pi + ai * pr + hi_ref[d, rows, lanes]
                hr_ref[d, rows, lanes] = nr
                hi_ref[d, rows, lanes] = ni
                pr, pi = nr, ni
            sr_ref[d, :, lanes] = pr
            si_ref[d, :, lanes] = pi

        for q in range(S5_Q):
            hs = slice(q * S5_QS, (q + 1) * S5_QS)
            cols = slice(q * LANE, (q + 1) * LANE)
            yq = (_dot(hr_ref[d, :, hs].astype(BF16), cw_ref[d, q, :S5_QS, :])
                  + _dot(hi_ref[d, :, hs].astype(BF16), cw_ref[d, q, S5_QS:, :]))
            if d == 0:
                yq = d_ref[:, cols] * ut_ref[d, q] + yq
            yt_ref[d, q] = yq
        for b in range(bsz):
            for q in range(S5_Q):
                c0 = b * S5_CH + q * LANE
                y_ref[:, c0:c0 + LANE] = yt_ref[d, q, pl.ds(b, tt, stride=bsz), :]

    fr_ref[...] = sr_ref[...]
    fi_ref[...] = si_ref[...]


def _s5_scan(u, d_skip, bw, cw, ar, ai, h0r, h0i, bsz, seq, tt, l):
    rows = tt * bsz
    nt = seq // tt
    fmap = lambda j: (j, 0)
    bmap = lambda j: (nt - 1 - j, 0)
    wsel = lambda j: (l, 0, 0, 0, 0)
    asel = lambda j: (l, 0, 0, 0)
    ssel = lambda j: (0, 0, 0)
    return pl.pallas_call(
        functools.partial(_s5_kernel, bsz=bsz, tt=tt),
        grid=(nt,),
        in_specs=[
            pl.BlockSpec((tt, bsz * S5_CH), fmap),
            pl.BlockSpec((tt, bsz * S5_CH), bmap),
            pl.BlockSpec((None, 1, S5_CH), lambda j: (l, 0, 0)),
            pl.BlockSpec((None, 2, S5_Q, LANE, 2 * S5_QS), wsel),
            pl.BlockSpec((None, 2, S5_Q, 2 * S5_QS, LANE), wsel),
            pl.BlockSpec((None, 2, 1, S5_NSTATE), asel),
            pl.BlockSpec((None, 2, 1, S5_NSTATE), asel),
            pl.BlockSpec((2, bsz, S5_NSTATE), ssel),
            pl.BlockSpec((2, bsz, S5_NSTATE), ssel),
        ],
        out_specs=[
            pl.BlockSpec((tt, bsz * S5_CH), fmap),
            pl.BlockSpec((tt, bsz * S5_CH), bmap),
            pl.BlockSpec((2, bsz, S5_NSTATE), ssel),
            pl.BlockSpec((2, bsz, S5_NSTATE), ssel),
        ],
        out_shape=[
            jax.ShapeDtypeStruct((seq, bsz * S5_CH), F32),
            jax.ShapeDtypeStruct((seq, bsz * S5_CH), F32),
            jax.ShapeDtypeStruct((2, bsz, S5_NSTATE), F32),
            jax.ShapeDtypeStruct((2, bsz, S5_NSTATE), F32),
        ],
        scratch_shapes=[
            pltpu.VMEM((2, S5_Q, rows, LANE), F32),
            pltpu.VMEM((2, S5_Q, rows, LANE), F32),
            pltpu.VMEM((2, rows, S5_NSTATE), F32),
            pltpu.VMEM((2, rows, S5_NSTATE), F32),
            pltpu.VMEM((2, bsz, S5_NSTATE), F32),
            pltpu.VMEM((2, bsz, S5_NSTATE), F32),
        ],
        compiler_params=_cparams(("arbitrary",)),
        name="s5_scan",
    )(u, u, d_skip, bw, cw, ar, ai, h0r, h0i)


def _s5_weights(lam_re, lam_im, log_dt, b_re, b_im, c_re, c_im):
    dt = jnp.exp(log_dt)[..., None]
    mag = jnp.exp(lam_re * dt)
    ab_re = mag * jnp.cos(lam_im * dt)
    ab_im = mag * jnp.sin(lam_im * dt)
    den = lam_re * lam_re + lam_im * lam_im
    nr = ab_re - 1.0
    f_re = (nr * lam_re + ab_im * lam_im) / den
    f_im = (ab_im * lam_re - nr * lam_im) / den
    bb_re = f_re[..., None] * b_re - f_im[..., None] * b_im
    bb_im = f_re[..., None] * b_im + f_im[..., None] * b_re
    eye = jnp.eye(8, dtype=F32)
    lshape = lam_re.shape[:2]

    def pack_b(bb):
        t = bb.reshape(*lshape, S5_Q, 8, S5_STATE, S5_GROUP)
        return jnp.einsum("ldqrpc,rs->ldqrcsp", t, eye).reshape(*lshape, S5_Q, LANE, S5_QS)

    def pack_c(cc):
        t = cc.reshape(*lshape, S5_Q, 8, S5_GROUP, S5_STATE)
        return jnp.einsum("ldqrcp,rs->ldqrpsc", t, eye).reshape(*lshape, S5_Q, S5_QS, LANE)

    bw = jnp.concatenate([pack_b(bb_re), pack_b(bb_im)], axis=-1).astype(BF16)
    cw = jnp.concatenate([pack_c(c_re), -pack_c(c_im)], axis=-2).astype(BF16)
    ar = ab_re.reshape(*lshape, 1, S5_NSTATE)
    ai = ab_im.reshape(*lshape, 1, S5_NSTATE)
    return bw, cw, ar, ai


def _outproj_kernel(x_ref, mod_ref, om_ref, yf_ref, yb_ref, og_ref, wglu_ref, bglu_ref, wout_ref, o_ref):
    y = yf_ref[...] + yb_ref[...]
    gate = _sigmoid(_dot(y.astype(BF16), wglu_ref[...]) + bglu_ref[...])
    gelu = 0.5 * y * (1.0 + jnp.tanh(math.sqrt(2.0 / math.pi) * (y + 0.044715 * (y * y * y))))
    s5 = (gelu * gate).astype(BF16)
    n_mla = MLA_HEADS * MLA_V
    mixed = (_dot(om_ref[...], wout_ref[0:n_mla, :])
             + _dot(s5, wout_ref[n_mla:n_mla + S5_CH, :])
             + _dot(og_ref[...], wout_ref[n_mla + S5_CH:, :]))
    o_ref[...] = x_ref[...] + mod_ref[5:6, :] * mixed


def _outproj(x, bsz, seq, mods_l, mod_base, per_batch_mod, o_mla, y_f, y_b, o_gqa, w, l):
    n = bsz * seq
    tm = min(OUTPROJ_TM, seq)
    nt = seq // tm
    row = lambda b, j: (b * nt + j, 0)
    lsel = lambda b, j: (l, 0, 0)
    mod_map = (lambda b, j: (mod_base + b, 0, 0)) if per_batch_mod else (lambda b, j: (mod_base, 0, 0))
    return pl.pallas_call(
        _outproj_kernel,
        grid=(bsz, nt),
        in_specs=[
            pl.BlockSpec((tm, D_MODEL), row),
            pl.BlockSpec((None, N_MOD, D_MODEL), mod_map),
            pl.BlockSpec((tm, MLA_HEADS * MLA_V), row),
            pl.BlockSpec((tm, S5_CH), lambda b, j: (j, b)),
            pl.BlockSpec((tm, S5_CH), lambda b, j: (j, b)),
            pl.BlockSpec((tm, GQA_Q_HEADS * GQA_HEAD_DIM), row),
            pl.BlockSpec((None, S5_CH, S5_CH), lsel),
            pl.BlockSpec((None, 1, S5_CH), lsel),
            pl.BlockSpec((None, D_MODEL, D_MODEL), lsel, pipeline_mode=RESIDENT),
        ],
        out_specs=pl.BlockSpec((tm, D_MODEL), row),
        out_shape=jax.ShapeDtypeStruct((n, D_MODEL), F32),
        compiler_params=_cparams(("parallel", "parallel")),
        name="outproj",
    )(x, mods_l, o_mla, y_f, y_b, o_gqa, w["s5_w_glu"], w["s5_b_glu"], w["w_out"])


def _rope_tables(seq, identity):
    if identity:
        one = jnp.ones((seq, LANE), F32)
        zero = jnp.zeros((seq, LANE), F32)
        return one, zero, zero, one, zero
    n_rows = seq // GRID_W
    row = jnp.broadcast_to(jnp.arange(n_rows, dtype=F32)[:, None], (n_rows, GRID_W)).reshape(seq)
    col = jnp.broadcast_to(jnp.arange(GRID_W, dtype=F32)[None, :], (n_rows, GRID_W)).reshape(seq)

    def angles(rot_dim):
        n_freq = rot_dim // 4
        inv = ROPE_THETA ** (-jnp.arange(n_freq, dtype=F32) / n_freq)
        ang = jnp.concatenate([row[:, None] * inv, col[:, None] * inv], axis=-1)
        return jnp.cos(ang), jnp.sin(ang)

    cm, sm = angles(MLA_ROPE)
    zero32 = jnp.zeros_like(sm)
    pad = jnp.zeros((seq, LANE - MLA_ROPE), F32)
    cos_m = jnp.concatenate([cm, cm, jnp.ones_like(pad)], axis=-1)
    sa = jnp.concatenate([zero32, sm, pad], axis=-1)
    sb = jnp.concatenate([-sm, zero32, pad], axis=-1)
    cg, sg = angles(GQA_HEAD_DIM)
    cos_g = jnp.concatenate([cg, cg], axis=-1)
    sin_g = jnp.concatenate([-sg, sg], axis=-1)
    return cos_m, sa, sb, cos_g, sin_g


def kernel(x_prompt, x_sample, cache_mla_ckv, cache_mla_krope, cache_gqa_k, cache_gqa_v, state_s5, c, c_ctx, w_ada, b_ada, norm_ffn1, ffn1_w_gate, ffn1_w_up, ffn1_w_down, norm_mix, w_in, mla_q_norm, mla_w_uq, mla_kv_norm, mla_w_ukv, s5_lambda_re, s5_lambda_im, s5_log_dt, s5_b_re, s5_b_im, s5_c_re, s5_c_im, s5_d, s5_w_glu, s5_b_glu, gqa_q_norm, gqa_k_norm, w_out, norm_ffn2, ffn2_w_gate, ffn2_w_up, ffn2_w_down, norm_final):
    L = DEPTH
    cb, cs, _ = x_prompt.shape
    lb, ls, _ = x_sample.shape
    past = cache_mla_ckv.shape[2]

    zpad = jnp.zeros((L, D_MODEL, LANE - MLA_ROPE), F32)
    w_in_p = jnp.concatenate([w_in[:, :, :MLA_Q_RANK + MLA_KV_RANK + MLA_ROPE], zpad,
                              w_in[:, :, MLA_Q_RANK + MLA_KV_RANK + MLA_ROPE:]], axis=-1).astype(BF16)
    wuq = mla_w_uq.reshape(L, MLA_Q_RANK, MLA_HEADS, MLA_NOPE + MLA_ROPE)
    wuq = jnp.pad(wuq, ((0, 0), (0, 0), (0, 0), (0, MLA_QK_PAD - MLA_NOPE - MLA_ROPE)))
    wuq = wuq.reshape(L, MLA_Q_RANK, MLA_HEADS * MLA_QK_PAD).astype(BF16)
    wukv = mla_w_ukv.reshape(L, MLA_KV_RANK, MLA_HEADS, MLA_NOPE + MLA_V)
    wukv = jnp.concatenate([wukv[..., :MLA_NOPE].reshape(L, MLA_KV_RANK, MLA_HEADS * MLA_NOPE),
                            wukv[..., MLA_NOPE:].reshape(L, MLA_KV_RANK, MLA_HEADS * MLA_V)], axis=-1).astype(BF16)
    w = {
        "norm_mix": norm_mix.reshape(L, 1, D_MODEL),
        "w_in": w_in_p,
        "mla_q_norm": mla_q_norm.reshape(L, 1, MLA_Q_RANK),
        "mla_w_uq": wuq,
        "mla_kv_norm": mla_kv_norm.reshape(L, 1, MLA_KV_RANK),
        "mla_w_ukv": wukv,
        "gqa_q_norm": gqa_q_norm.reshape(L, 1, GQA_HEAD_DIM),
        "gqa_k_norm": gqa_k_norm.reshape(L, 1, GQA_HEAD_DIM),
        "s5_w_glu": s5_w_glu.astype(BF16),
        "s5_b_glu": s5_b_glu.reshape(L, 1, S5_CH),
        "w_out": w_out.astype(BF16),
    }
    f1 = (norm_ffn1.reshape(L, 1, D_MODEL), ffn1_w_gate.astype(BF16), ffn1_w_up.astype(BF16), ffn1_w_down.astype(BF16))
    f2 = (norm_ffn2.reshape(L, 1, D_MODEL), ffn2_w_gate.astype(BF16), ffn2_w_up.astype(BF16), ffn2_w_down.astype(BF16))
    bw, cw, ar, ai = _s5_weights(s5_lambda_re, s5_lambda_im, s5_log_dt, s5_b_re, s5_b_im, s5_c_re, s5_c_im)
    d_skip = s5_d.reshape(L, 1, S5_CH)
    rope_ctx = _rope_tables(256, True)
    rope_lat = _rope_tables(ls, False)
    cache_kr_pad = jnp.pad(cache_mla_krope, ((0, 0), (0, 0), (0, 0), (0, LANE - MLA_ROPE)))
    cache_k = cache_gqa_k.reshape(lb, L, past, GQA_KV_HEADS * GQA_HEAD_DIM)
    cache_vt = jnp.swapaxes(cache_gqa_v.reshape(lb, L, past, GQA_KV_HEADS * GQA_HEAD_DIM), 2, 3).astype(BF16)
    h0 = state_s5.reshape(lb, L, 2, S5_NSTATE, 2)
    zero_state = jnp.zeros((2, cb, S5_NSTATE), F32)

    cmat = jnp.concatenate([c_ctx[None, :], c, jnp.zeros((MOD_ROWS - 1 - lb, D_MODEL), F32)], axis=0)
    mods = _modulation(cmat, w_ada, b_ada)

    xp = x_prompt.reshape(cb * cs, D_MODEL)
    xs = x_sample.reshape(lb * ls, D_MODEL)
    st_ckv, st_kr, st_k, st_v, st_s5 = [], [], [], [], []

    def mixer(x, bsz, seq, mods_l, mod_base, per_batch_mod, l, rope, is_ctx):
        outs = _inproj(x, bsz, seq, mods_l, mod_base, per_batch_mod, w, l, rope, is_ctx)
        qm, km, vm, u_tm, gq, gk, gv = outs[:7]
        grep = GQA_Q_HEADS // GQA_KV_HEADS
        mla_hps, gqa_hps = (MLA_HEADS, GQA_Q_HEADS) if is_ctx else (ATTN_LAT_HPS, ATTN_LAT_HPS)
        mla_parts = [_kv_part_2d(km, vm, seq, MLA_QK_PAD, MLA_V, MLA_HEADS, mla_hps)]
        gqa_parts = [_kv_part_2d(gk, gv, seq, GQA_HEAD_DIM, GQA_HEAD_DIM, GQA_KV_HEADS, gqa_hps // grep)]
        if is_ctx:
            h0r = h0i = zero_state
            tq, tt = 256, 32
        else:
            km_c, vm_c = _kvup(cache_mla_ckv, cache_kr_pad, w["mla_w_ukv"], l)
            mla_parts = [_kv_part_2d(km_c, vm_c, past, MLA_QK_PAD, MLA_V, MLA_HEADS, mla_hps)] + mla_parts
            gqa_parts = [_kv_part_cache(cache_k, cache_vt, past, GQA_HEAD_DIM, GQA_HEAD_DIM, gqa_hps // grep, l)] + gqa_parts
            h0r = jnp.swapaxes(h0[:, l, :, :, 0], 0, 1)
            h0i = jnp.swapaxes(h0[:, l, :, :, 1], 0, 1)
            tq, tt = 1024, 64
        o_mla = _attention(qm, mla_parts, bsz, seq, MLA_HEADS, 1, MLA_QK_PAD, MLA_V, tq, mla_hps)
        o_gqa = _attention(gq, gqa_parts, bsz, seq, GQA_Q_HEADS, grep, GQA_HEAD_DIM, GQA_HEAD_DIM, tq, gqa_hps)
        y_f, y_b, fin_r, fin_i = _s5_scan(u_tm, d_skip, bw, cw, ar, ai, h0r, h0i, bsz, seq, tt, l)
        x = _outproj(x, bsz, seq, mods_l, mod_base, per_batch_mod, o_mla, y_f, y_b, o_gqa, w, l)
        extra = None
        if is_ctx:
            ckv, kr, kg, vg = outs[7:]
            s5s = jnp.swapaxes(jnp.stack([fin_r, fin_i], axis=-1), 0, 1)
            extra = (ckv.reshape(bsz, seq, MLA_KV_RANK), kr.reshape(bsz, seq, MLA_ROPE),
                     kg.reshape(bsz, seq, GQA_KV_HEADS, GQA_HEAD_DIM), vg.reshape(bsz, seq, GQA_KV_HEADS, GQA_HEAD_DIM),
                     s5s.reshape(bsz, 2, S5_GROUPS, S5_STATE, 2))
        return x, extra

    g_final = norm_final.reshape(1, D_MODEL)
    for l in range(L):
        mods_l = mods[l]
        fin = g_final if l == L - 1 else None
        xp = _ffn(xp, mods_l, 0, cb * cs, *f1, l, 0)
        xp, (ckv, kr, kg, vg, s5s) = mixer(xp, cb, cs, mods_l, 0, False, l, rope_ctx, True)
        xp = _ffn(xp, mods_l, 0, cb * cs, *f2, l, 6, fin)
        st_ckv.append(ckv)
        st_kr.append(kr)
        st_k.append(kg)
        st_v.append(vg)
        st_s5.append(s5s)
        xs = _ffn(xs, mods_l, 1, ls, *f1, l, 0)
        xs, _ = mixer(xs, lb, ls, mods_l, 1, True, l, rope_lat, False)
        xs = _ffn(xs, mods_l, 1, ls, *f2, l, 6, fin)

    y_prompt = xp.reshape(cb, cs, D_MODEL)
    y_sample = xs.reshape(lb, ls, D_MODEL)
    return (y_prompt, y_sample, jnp.stack(st_ckv, axis=1), jnp.stack(st_kr, axis=1), jnp.stack(st_k, axis=1),
            jnp.stack(st_v, axis=1), jnp.stack(st_s5, axis=1))
```

```python
import functools
import math

import jax
import jax.numpy as jnp
from jax import lax
from jax.experimental import pallas as pl
from jax.experimental.pallas import tpu as pltpu

D_MODEL = 2048
DEPTH = 4
GRID_W = 64
ROPE_THETA = 10000.0
EPS = 1e-6
N_MOD = 9
D_FF = 2 * D_MODEL

MLA_HEADS = 8
MLA_Q_RANK = D_MODEL // 4
MLA_KV_RANK = D_MODEL // 8
MLA_NOPE = 128
MLA_ROPE = 64
MLA_V = 128
MLA_QK_PAD = 256

S5_CH = D_MODEL // 4
S5_GROUP = 16
S5_GROUPS = S5_CH // S5_GROUP
S5_STATE = 64
S5_NSTATE = S5_GROUPS * S5_STATE

GQA_Q_HEADS = 4
GQA_KV_HEADS = 2
GQA_HEAD_DIM = 128

MLA_SCALE = 1.0 / math.sqrt(MLA_NOPE + MLA_ROPE)
GQA_SCALE = 1.0 / math.sqrt(GQA_HEAD_DIM)
LOG2E = math.log2(math.e)
MLA_QSCALE = MLA_SCALE * LOG2E
GQA_QSCALE = GQA_SCALE * LOG2E

ZQ0, ZKV0, ZKR0, ZU0, ZGQ0, ZGK0, ZGV0, Z_END = 0, 512, 768, 896, 1408, 1920, 2176, 2432

LANE = 128
SUBLANE = 8
MOD_ROWS = 16

F32 = jnp.float32
BF16 = jnp.bfloat16

VMEM_LIMIT = 56 * 1024 * 1024
INPROJ_TM = 256
OUTPROJ_TM = 512
RESIDENT = pl.Buffered(1)


def _cparams(sem):
    return pltpu.CompilerParams(dimension_semantics=sem, vmem_limit_bytes=VMEM_LIMIT)


def _rms(x, g):
    return x * lax.rsqrt(jnp.mean(x * x, axis=-1, keepdims=True) + EPS) * g


def _sigmoid(x):
    return 1.0 / (1.0 + jnp.exp(-x))


def _dot(a, b):
    return jnp.dot(a, b, preferred_element_type=F32)


def _mod_kernel(c_ref, w_ref, b_ref, o_ref):
    c = c_ref[...]
    a = (c * _sigmoid(c)).astype(BF16)
    o_ref[...] = _dot(a, w_ref[...].astype(BF16)) + b_ref[...]


def _modulation(cmat, w_ada, b_ada):
    tn = 1024
    nd = N_MOD * D_MODEL
    out = pl.pallas_call(
        _mod_kernel,
        grid=(DEPTH, nd // tn),
        in_specs=[
            pl.BlockSpec((MOD_ROWS, D_MODEL), lambda l, j: (0, 0)),
            pl.BlockSpec((None, D_MODEL, tn), lambda l, j: (l, 0, j)),
            pl.BlockSpec((None, 1, tn), lambda l, j: (l, 0, j)),
        ],
        out_specs=pl.BlockSpec((None, MOD_ROWS, tn), lambda l, j: (l, 0, j)),
        out_shape=jax.ShapeDtypeStruct((DEPTH, MOD_ROWS, nd), F32),
        compiler_params=_cparams(("parallel", "parallel")),
        name="modulation",
    )(cmat, w_ada, b_ada.reshape(DEPTH, 1, nd))
    return out.reshape(DEPTH, MOD_ROWS, N_MOD, D_MODEL)


def _ffn_kernel(x_ref, mod_ref, g_ref, wg_hbm, wu_hbm, wd_hbm, *rest, k0, final, l):
    if final:
        gf_ref, o_ref, wg_buf, wu_buf, wd_buf, hm_ref, sem = rest
    else:
        o_ref, wg_buf, wu_buf, wd_buf, hm_ref, sem = rest
    i = pl.program_id(0)
    nff = D_FF // FFN_TF

    def tile_copies(j, slot):
        cols = pl.ds(j * FFN_TF, FFN_TF)
        return (pltpu.make_async_copy(wg_hbm.at[l, :, cols], wg_buf.at[slot], sem.at[0, slot]),
                pltpu.make_async_copy(wu_hbm.at[l, :, cols], wu_buf.at[slot], sem.at[1, slot]),
                pltpu.make_async_copy(wd_hbm.at[l, cols, :], wd_buf.at[slot], sem.at[2, slot]))

    @pl.when(i == 0)
    def _():
        for c in tile_copies(0, 0):
            c.start()

    hm_ref[...] = (_rms(x_ref[...], g_ref[...]) * (1.0 + mod_ref[k0 + 1:k0 + 2, :])
                   + mod_ref[k0:k0 + 1, :]).astype(BF16)
    for j in range(nff):
        slot = j % 2
        if j + 1 < nff:
            for c in tile_copies(j + 1, 1 - slot):
                c.start()
        else:
            @pl.when(i + 1 < pl.num_programs(0))
            def _():
                for c in tile_copies(0, 1 - slot):
                    c.start()
        for c in tile_copies(j, slot):
            c.wait()
        for c0 in range(0, FFN_TF, FFN_SUB):
            h = hm_ref[...]
            g = _dot(h, wg_buf[slot, :, c0:c0 + FFN_SUB])
            u = _dot(h, wu_buf[slot, :, c0:c0 + FFN_SUB])
            a = (g * _sigmoid(g)) * u
            d = _dot(a.astype(BF16), wd_buf[slot, c0:c0 + FFN_SUB, :])
            if j == 0 and c0 == 0:
                o_ref[...] = d
            else:
                o_ref[...] += d
    out = x_ref[...] + (0.5 * mod_ref[k0 + 2:k0 + 3, :]) * o_ref[...]
    if final:
        out = _rms(out, gf_ref[...])
    o_ref[...] = out


FFN_TM = 512
FFN_TF = 1024
FFN_SUB = 512
assert (D_FF // FFN_TF) % 2 == 0


def _ffn(x, mods_l, mod_base, rows_per_mod, norm_g, wg, wu, wd, l, k0, final_g=None):
    n = x.shape[0]
    tm = FFN_TM
    in_specs = [
        pl.BlockSpec((tm, D_MODEL), lambda i: (i, 0)),
        pl.BlockSpec((None, N_MOD, D_MODEL), lambda i: (mod_base + (i * tm) // rows_per_mod, 0, 0)),
        pl.BlockSpec((None, 1, D_MODEL), lambda i: (l, 0, 0)),
        pl.BlockSpec(memory_space=pl.ANY),
        pl.BlockSpec(memory_space=pl.ANY),
        pl.BlockSpec(memory_space=pl.ANY),
    ]
    args = [x, mods_l, norm_g, wg, wu, wd]
    if final_g is not None:
        in_specs.append(pl.BlockSpec((1, D_MODEL), lambda i: (0, 0)))
        args.append(final_g)
    return pl.pallas_call(
        functools.partial(_ffn_kernel, k0=k0, final=final_g is not None, l=l),
        grid=(n // tm,),
        in_specs=in_specs,
        out_specs=pl.BlockSpec((tm, D_MODEL), lambda i: (i, 0)),
        out_shape=jax.ShapeDtypeStruct((n, D_MODEL), F32),
        scratch_shapes=[
            pltpu.VMEM((2, D_MODEL, FFN_TF), BF16),
            pltpu.VMEM((2, D_MODEL, FFN_TF), BF16),
            pltpu.VMEM((2, FFN_TF, D_MODEL), BF16),
            pltpu.VMEM((tm, D_MODEL), BF16),
            pltpu.SemaphoreType.DMA((3, 2)),
        ],
        compiler_params=_cparams(("arbitrary",)),
        name="ffn",
    )(*args)


def _rope_mla(t, cos, sa, sb):
    return t * cos + pltpu.roll(t, 32, 1) * sa + pltpu.roll(t, 96, 1) * sb


def _rope_gqa(t, cos, ss):
    return t * cos + pltpu.roll(t, 64, 1) * ss


def _inproj_kernel(x_ref, mod_ref, g_ref, win_ref, qn_ref, wuq_ref, kvn_ref, wukv_ref, gqn_ref, gkn_ref,
                   cm_ref, sa_ref, sb_ref, cg_ref, sg_ref,
                   qm_ref, km_ref, vm_ref, u_ref, gq_ref, gk_ref, gv_ref, *cache_refs):
    x = x_ref[...]
    h = (_rms(x, g_ref[...]) * (1.0 + mod_ref[4:5, :]) + mod_ref[3:4, :]).astype(BF16)
    z = _dot(h, win_ref[...])
    cm, sa, sb = cm_ref[...], sa_ref[...], sb_ref[...]
    cg, sg = cg_ref[...], sg_ref[...]

    ckv = _rms(z[:, ZKV0:ZKR0], kvn_ref[...])
    kr = _rope_mla(z[:, ZKR0:ZU0], cm, sa, sb)
    kr_b = kr.astype(BF16)

    q = _dot(_rms(z[:, ZQ0:ZKV0], qn_ref[...]).astype(BF16), wuq_ref[...])
    kv = _dot(ckv.astype(BF16), wukv_ref[...])
    for hd in range(MLA_HEADS):
        c0 = hd * MLA_QK_PAD
        qm_ref[:, c0:c0 + LANE] = (q[:, c0:c0 + LANE] * MLA_QSCALE).astype(BF16)
        qm_ref[:, c0 + LANE:c0 + 2 * LANE] = (
            _rope_mla(q[:, c0 + LANE:c0 + 2 * LANE], cm, sa, sb) * MLA_QSCALE).astype(BF16)
        km_ref[:, c0:c0 + LANE] = kv[:, hd * LANE:(hd + 1) * LANE].astype(BF16)
        km_ref[:, c0 + LANE:c0 + 2 * LANE] = kr_b
    vm_ref[...] = kv[:, MLA_HEADS * MLA_NOPE:].T.astype(BF16)

    u_ref[...] = z[:, ZU0:ZGQ0]

    for hd in range(GQA_Q_HEADS):
        t = _rms(z[:, ZGQ0 + hd * LANE:ZGQ0 + (hd + 1) * LANE], gqn_ref[...])
        gq_ref[:, hd * LANE:(hd + 1) * LANE] = (_rope_gqa(t, cg, sg) * GQA_QSCALE).astype(BF16)
    kgs = []
    for hd in range(GQA_KV_HEADS):
        t = _rms(z[:, ZGK0 + hd * LANE:ZGK0 + (hd + 1) * LANE], gkn_ref[...])
        t = _rope_gqa(t, cg, sg)
        kgs.append(t)
        gk_ref[:, hd * LANE:(hd + 1) * LANE] = t.astype(BF16)
    gv = z[:, ZGV0:Z_END]
    gv_ref[...] = gv.T.astype(BF16)

    if cache_refs:
        ckv_o, kr_o, kg_o, vg_o = cache_refs
        ckv_o[...] = ckv
        kr_o[...] = kr[:, :MLA_ROPE]
        for hd in range(GQA_KV_HEADS):
            kg_o[:, hd * LANE:(hd + 1) * LANE] = kgs[hd]
        vg_o[...] = gv


def _inproj(x, bsz, seq, mods_l, mod_base, per_batch_mod, w, l, rope, emit_cache):
    n = bsz * seq
    tm = min(INPROJ_TM, seq)
    nt = seq // tm
    cm, sa, sb, cg, sg = rope
    rope_blocks = cm.shape[0] // tm
    rmap = (lambda b, j: (j, 0)) if rope_blocks > 1 else (lambda b, j: (0, 0))
    row = lambda b, j: (b * nt + j, 0)
    lsel = lambda b, j: (l, 0, 0)
    mod_map = (lambda b, j: (mod_base + b, 0, 0)) if per_batch_mod else (lambda b, j: (mod_base, 0, 0))
    out_shape = [
        jax.ShapeDtypeStruct((n, MLA_HEADS * MLA_QK_PAD), BF16),
        jax.ShapeDtypeStruct((n, MLA_HEADS * MLA_QK_PAD), BF16),
        jax.ShapeDtypeStruct((bsz * MLA_HEADS * MLA_V, seq), BF16),
        jax.ShapeDtypeStruct((seq, bsz * S5_CH), F32),
        jax.ShapeDtypeStruct((n, GQA_Q_HEADS * GQA_HEAD_DIM), BF16),
        jax.ShapeDtypeStruct((n, GQA_KV_HEADS * GQA_HEAD_DIM), BF16),
        jax.ShapeDtypeStruct((bsz * GQA_KV_HEADS * GQA_HEAD_DIM, seq), BF16),
    ]
    out_specs = [
        pl.BlockSpec((tm, MLA_HEADS * MLA_QK_PAD), row),
        pl.BlockSpec((tm, MLA_HEADS * MLA_QK_PAD), row),
        pl.BlockSpec((MLA_HEADS * MLA_V, tm), lambda b, j: (b, j)),
        pl.BlockSpec((tm, S5_CH), lambda b, j: (j, b)),
        pl.BlockSpec((tm, GQA_Q_HEADS * GQA_HEAD_DIM), row),
        pl.BlockSpec((tm, GQA_KV_HEADS * GQA_HEAD_DIM), row),
        pl.BlockSpec((GQA_KV_HEADS * GQA_HEAD_DIM, tm), lambda b, j: (b, j)),
    ]
    if emit_cache:
        out_shape += [
            jax.ShapeDtypeStruct((n, MLA_KV_RANK), F32),
            jax.ShapeDtypeStruct((n, MLA_ROPE), F32),
            jax.ShapeDtypeStruct((n, GQA_KV_HEADS * GQA_HEAD_DIM), F32),
            jax.ShapeDtypeStruct((n, GQA_KV_HEADS * GQA_HEAD_DIM), F32),
        ]
        out_specs += [
            pl.BlockSpec((tm, MLA_KV_RANK), row),
            pl.BlockSpec((tm, MLA_ROPE), row),
            pl.BlockSpec((tm, GQA_KV_HEADS * GQA_HEAD_DIM), row),
            pl.BlockSpec((tm, GQA_KV_HEADS * GQA_HEAD_DIM), row),
        ]
    return pl.pallas_call(
        _inproj_kernel,
        grid=(bsz, nt),
        in_specs=[
            pl.BlockSpec((tm, D_MODEL), row),
            pl.BlockSpec((None, N_MOD, D_MODEL), mod_map),
            pl.BlockSpec((None, 1, D_MODEL), lsel),
            pl.BlockSpec((None, D_MODEL, Z_END), lsel, pipeline_mode=RESIDENT),
            pl.BlockSpec((None, 1, MLA_Q_RANK), lsel),
            pl.BlockSpec((None, MLA_Q_RANK, MLA_HEADS * MLA_QK_PAD), lsel, pipeline_mode=RESIDENT),
            pl.BlockSpec((None, 1, MLA_KV_RANK), lsel),
            pl.BlockSpec((None, MLA_KV_RANK, MLA_HEADS * (MLA_NOPE + MLA_V)), lsel, pipeline_mode=RESIDENT),
            pl.BlockSpec((None, 1, GQA_HEAD_DIM), lsel),
            pl.BlockSpec((None, 1, GQA_HEAD_DIM), lsel),
        ] + [pl.BlockSpec((tm, LANE), rmap)] * 5,
        out_specs=out_specs,
        out_shape=out_shape,
        compiler_params=_cparams(("parallel", "parallel")),
        name="inproj",
    )(x, mods_l, w["norm_mix"], w["w_in"], w["mla_q_norm"], w["mla_w_uq"], w["mla_kv_norm"], w["mla_w_ukv"],
      w["gqa_q_norm"], w["gqa_k_norm"], cm, sa, sb, cg, sg)


def _kvup_kernel(ckv_ref, kr_ref, wukv_ref, km_ref, vm_ref):
    kv = _dot(ckv_ref[...].astype(BF16), wukv_ref[...])
    kr_b = kr_ref[...].astype(BF16)
    for hd in range(MLA_HEADS):
        c0 = hd * MLA_QK_PAD
        km_ref[:, c0:c0 + LANE] = kv[:, hd * LANE:(hd + 1) * LANE].astype(BF16)
        km_ref[:, c0 + LANE:c0 + 2 * LANE] = kr_b
    vm_ref[...] = kv[:, MLA_HEADS * MLA_NOPE:].T.astype(BF16)


def _kvup(cache_ckv, cache_kr_pad, wukv, l):
    bsz, _, past, _ = cache_ckv.shape
    n = bsz * past
    return pl.pallas_call(
        _kvup_kernel,
        grid=(bsz,),
        in_specs=[
            pl.BlockSpec((None, None, past, MLA_KV_RANK), lambda b: (b, l, 0, 0)),
            pl.BlockSpec((None, None, past, LANE), lambda b: (b, l, 0, 0)),
            pl.BlockSpec((None, MLA_KV_RANK, MLA_HEADS * (MLA_NOPE + MLA_V)), lambda b: (l, 0, 0)),
        ],
        out_specs=[
            pl.BlockSpec((past, MLA_HEADS * MLA_QK_PAD), lambda b: (b, 0)),
            pl.BlockSpec((MLA_HEADS * MLA_V, past), lambda b: (b, 0)),
        ],
        out_shape=[
            jax.ShapeDtypeStruct((n, MLA_HEADS * MLA_QK_PAD), BF16),
            jax.ShapeDtypeStruct((bsz * MLA_HEADS * MLA_V, past), BF16),
        ],
        compiler_params=_cparams(("parallel",)),
        name="kvup",
    )(cache_ckv, cache_kr_pad, wukv)


ATTN_CK = 256
ATTN_SUB = 512
ATTN_LAT_HPS = 2


def _attn_kernel(*refs, nparts, hps, rep, dk, dv):
    q_ref = refs[0]
    k_refs = refs[1:1 + nparts]
    vt_refs = refs[1 + nparts:1 + 2 * nparts]
    o_ref = refs[1 + 2 * nparts]
    s_ref = refs[2 + 2 * nparts]
    tq = q_ref.shape[0]
    ts = min(tq, ATTN_SUB)
    dn = (((1,), (1,)), ((), ()))
    chunks = []
    off = 0
    for pi, k_ref in enumerate(k_refs):
        rows = k_ref.shape[0]
        for r0 in range(0, rows, ATTN_CK):
            ck = min(ATTN_CK, rows - r0)
            chunks.append((pi, r0, ck, off))
            off += ck

    items = [(hh, q0) for hh in range(hps) for q0 in range(0, tq, ts)]

    maxes = []
    for hh, q0 in items:
        kv = hh // rep
        q = q_ref[q0:q0 + ts, hh * dk:(hh + 1) * dk]
        mpart = None
        for pi, r0, ck, off in chunks:
            k = k_refs[pi][r0:r0 + ck, kv * dk:(kv + 1) * dk].astype(BF16)
            s = lax.dot_general(k, q, dn, preferred_element_type=F32)
            s_ref[hh, off:off + ck, q0:q0 + ts] = s
            blk = jnp.max(s.reshape(ck // SUBLANE, SUBLANE, ts), axis=0)
            mpart = blk if mpart is None else jnp.maximum(mpart, blk)
        maxes.append(jnp.max(mpart, axis=0, keepdims=True))

    for (hh, q0), m in zip(items, maxes):
        kv = hh // rep
        lpart = jnp.zeros((SUBLANE, ts), F32)
        acc = jnp.zeros((dv, ts), F32)
        for pi, r0, ck, off in chunks:
            p = jnp.exp2(s_ref[hh, off:off + ck, q0:q0 + ts] - m)
            lpart = lpart + jnp.sum(p.reshape(ck // SUBLANE, SUBLANE, ts), axis=0)
            vt = vt_refs[pi][kv * dv:(kv + 1) * dv, r0:r0 + ck].astype(BF16)
            acc = acc + _dot(vt, p.astype(BF16))
        den = jnp.sum(lpart, axis=0, keepdims=True)
        o_ref[q0:q0 + ts, hh * dv:(hh + 1) * dv] = (acc / den).T.astype(o_ref.dtype)


def _attention(q, parts, bsz, seq, heads, rep, dk, dv, tq, hps):
    nq = seq // tq
    n_keys = sum(p[4] for p in parts)
    assert all(p[4] % LANE == 0 for p in parts) and heads % hps == 0 and hps % rep == 0
    in_specs = [pl.BlockSpec((tq, hps * dk), lambda b, h, i: (b * nq + i, h))]
    in_specs += [p[2] for p in parts] + [p[3] for p in parts]
    args = [q] + [p[0] for p in parts] + [p[1] for p in parts]
    return pl.pallas_call(
        functools.partial(_attn_kernel, nparts=len(parts), hps=hps, rep=rep, dk=dk, dv=dv),
        grid=(bsz, heads // hps, nq),
        in_specs=in_specs,
        out_specs=pl.BlockSpec((tq, hps * dv), lambda b, h, i: (b * nq + i, h)),
        out_shape=jax.ShapeDtypeStruct((bsz * seq, heads * dv), BF16),
        scratch_shapes=[pltpu.VMEM((hps, n_keys, tq), F32)],
        compiler_params=_cparams(("parallel", "parallel", "arbitrary")),
        name="attention",
    )(*args)


def _kv_part_2d(k, vt, rows, dk, dv, kv_heads, kps):
    ng = kv_heads // kps
    return (k, vt,
            pl.BlockSpec((rows, kps * dk), lambda b, h, i: (b, h)),
            pl.BlockSpec((kps * dv, rows), lambda b, h, i: (b * ng + h, 0)), rows)


def _kv_part_cache(k, vt, rows, dk, dv, kps, l):
    return (k, vt,
            pl.BlockSpec((None, None, rows, kps * dk), lambda b, h, i: (b, l, 0, h)),
            pl.BlockSpec((None, None, kps * dv, rows), lambda b, h, i: (b, l, h, 0)), rows)


S5_LC = 512
S5_Q = S5_CH // LANE
S5_QS = S5_NSTATE // S5_Q


def _s5_kernel(uf_ref, ub_ref, d_ref, bw_ref, cw_ref, ar_ref, ai_ref, h0r_ref, h0i_ref,
               yf_ref, yb_ref, fr_ref, fi_ref, ut_ref, yt_ref, hr_ref, hi_ref, sr_ref, si_ref, *, bsz, tt):
    j = pl.program_id(0)

    @pl.when(j == 0)
    def _():
        sr_ref[...] = h0r_ref[...]
        si_ref[...] = h0i_ref[...]

    for d, (u_ref, y_ref) in enumerate(((uf_ref, yf_ref), (ub_ref, yb_ref))):
        for b in range(bsz):
            for q in range(S5_Q):
                c0 = b * S5_CH + q * LANE
                ut_ref[d, q, pl.ds(b, tt, stride=bsz), :] = u_ref[:, c0:c0 + LANE]
        for q in range(S5_Q):
            bu = _dot(ut_ref[d, q].astype(BF16), bw_ref[d, q])
            hr_ref[d, :, q * S5_QS:(q + 1) * S5_QS] = bu[:, :S5_QS]
            hi_ref[d, :, q * S5_QS:(q + 1) * S5_QS] = bu[:, S5_QS:]

        for c in range(S5_NSTATE // S5_LC):
            lanes = slice(c * S5_LC, (c + 1) * S5_LC)
            ar = jnp.broadcast_to(ar_ref[d, :, lanes], (bsz, S5_LC))
            ai = jnp.broadcast_to(ai_ref[d, :, lanes], (bsz, S5_LC))
            pr = sr_ref[d, :, lanes]
            pi = si_ref[d, :, lanes]
            for t in (range(tt - 1, -1, -1) if d == 1 else range(tt)):
                rows = slice(t * bsz, (t + 1) * bsz)
                nr = ar * pr - ai * pi + hr_ref[d, rows, lanes]
                ni = ar * pi + ai * pr + hi_ref[d, rows, lanes]
                hr_ref[d, rows, lanes] = nr
                hi_ref[d, rows, lanes] = ni
                pr, pi = nr, ni
            sr_ref[d, :, lanes] = pr
            si_ref[d, :, lanes] = pi

        for q in range(S5_Q):
            hs = slice(q * S5_QS, (q + 1) * S5_QS)
            cols = slice(q * LANE, (q + 1) * LANE)
            yq = (_dot(hr_ref[d, :, hs].astype(BF16), cw_ref[d, q, :S5_QS, :])
                  + _dot(hi_ref[d, :, hs].astype(BF16), cw_ref[d, q, S5_QS:, :]))
            if d == 0:
                yq = d_ref[:, cols] * ut_ref[d, q] + yq
            yt_ref[d, q] = yq
        for b in range(bsz):
            for q in range(S5_Q):
                c0 = b * S5_CH + q * LANE
                y_ref[:, c0:c0 + LANE] = yt_ref[d, q, pl.ds(b, tt, stride=bsz), :]

    fr_ref[...] = sr_ref[...]
    fi_ref[...] = si_ref[...]


def _s5_scan(u, d_skip, bw, cw, ar, ai, h0r, h0i, bsz, seq, tt, l):
    rows = tt * bsz
    nt = seq // tt
    fmap = lambda j: (j, 0)
    bmap = lambda j: (nt - 1 - j, 0)
    wsel = lambda j: (l, 0, 0, 0, 0)
    asel = lambda j: (l, 0, 0, 0)
    ssel = lambda j: (0, 0, 0)
    return pl.pallas_call(
        functools.partial(_s5_kernel, bsz=bsz, tt=tt),
        grid=(nt,),
        in_specs=[
            pl.BlockSpec((tt, bsz * S5_CH), fmap),
            pl.BlockSpec((tt, bsz * S5_CH), bmap),
            pl.BlockSpec((None, 1, S5_CH), lambda j: (l, 0, 0)),
            pl.BlockSpec((None, 2, S5_Q, LANE, 2 * S5_QS), wsel),
            pl.BlockSpec((None, 2, S5_Q, 2 * S5_QS, LANE), wsel),
            pl.BlockSpec((None, 2, 1, S5_NSTATE), asel),
            pl.BlockSpec((None, 2, 1, S5_NSTATE), asel),
            pl.BlockSpec((2, bsz, S5_NSTATE), ssel),
            pl.BlockSpec((2, bsz, S5_NSTATE), ssel),
        ],
        out_specs=[
            pl.BlockSpec((tt, bsz * S5_CH), fmap),
            pl.BlockSpec((tt, bsz * S5_CH), bmap),
            pl.BlockSpec((2, bsz, S5_NSTATE), ssel),
            pl.BlockSpec((2, bsz, S5_NSTATE), ssel),
        ],
        out_shape=[
            jax.ShapeDtypeStruct((seq, bsz * S5_CH), F32),
            jax.ShapeDtypeStruct((seq, bsz * S5_CH), F32),
            jax.ShapeDtypeStruct((2, bsz, S5_NSTATE), F32),
            jax.ShapeDtypeStruct((2, bsz, S5_NSTATE), F32),
        ],
        scratch_shapes=[
            pltpu.VMEM((2, S5_Q, rows, LANE), F32),
            pltpu.VMEM((2, S5_Q, rows, LANE), F32),
            pltpu.VMEM((2, rows, S5_NSTATE), F32),
            pltpu.VMEM((2, rows, S5_NSTATE), F32),
            pltpu.VMEM((2, bsz, S5_NSTATE), F32),
            pltpu.VMEM((2, bsz, S5_NSTATE), F32),
        ],
        compiler_params=_cparams(("arbitrary",)),
        name="s5_scan",
    )(u, u, d_skip, bw, cw, ar, ai, h0r, h0i)


def _s5_weights(lam_re, lam_im, log_dt, b_re, b_im, c_re, c_im):
    dt = jnp.exp(log_dt)[..., None]
    mag = jnp.exp(lam_re * dt)
    ab_re = mag * jnp.cos(lam_im * dt)
    ab_im = mag * jnp.sin(lam_im * dt)
    den = lam_re * lam_re + lam_im * lam_im
    nr = ab_re - 1.0
    f_re = (nr * lam_re + ab_im * lam_im) / den
    f_im = (ab_im * lam_re - nr * lam_im) / den
    bb_re = f_re[..., None] * b_re - f_im[..., None] * b_im
    bb_im = f_re[..., None] * b_im + f_im[..., None] * b_re
    eye = jnp.eye(8, dtype=F32)
    lshape = lam_re.shape[:2]

    def pack_b(bb):
        t = bb.reshape(*lshape, S5_Q, 8, S5_STATE, S5_GROUP)
        return jnp.einsum("ldqrpc,rs->ldqrcsp", t, eye).reshape(*lshape, S5_Q, LANE, S5_QS)

    def pack_c(cc):
        t = cc.reshape(*lshape, S5_Q, 8, S5_GROUP, S5_STATE)
        return jnp.einsum("ldqrcp,rs->ldqrpsc", t, eye).reshape(*lshape, S5_Q, S5_QS, LANE)

    bw = jnp.concatenate([pack_b(bb_re), pack_b(bb_im)], axis=-1).astype(BF16)
    cw = jnp.concatenate([pack_c(c_re), -pack_c(c_im)], axis=-2).astype(BF16)
    ar = ab_re.reshape(*lshape, 1, S5_NSTATE)
    ai = ab_im.reshape(*lshape, 1, S5_NSTATE)
    return bw, cw, ar, ai


def _outproj_kernel(x_ref, mod_ref, om_ref, yf_ref, yb_ref, og_ref, wglu_ref, bglu_ref, wout_ref, o_ref):
    y = yf_ref[...] + yb_ref[...]
    gate = _sigmoid(_dot(y.astype(BF16), wglu_ref[...]) + bglu_ref[...])
    gelu = 0.5 * y * (1.0 + jnp.tanh(math.sqrt(2.0 / math.pi) * (y + 0.044715 * (y * y * y))))
    s5 = (gelu * gate).astype(BF16)
    n_mla = MLA_HEADS * MLA_V
    mixed = (_dot(om_ref[...], wout_ref[0:n_mla, :])
             + _dot(s5, wout_ref[n_mla:n_mla + S5_CH, :])
             + _dot(og_ref[...], wout_ref[n_mla + S5_CH:, :]))
    o_ref[...] = x_ref[...] + mod_ref[5:6, :] * mixed


def _outproj(x, bsz, seq, mods_l, mod_base, per_batch_mod, o_mla, y_f, y_b, o_gqa, w, l):
    n = bsz * seq
    tm = min(OUTPROJ_TM, seq)
    nt = seq // tm
    row = lambda b, j: (b * nt + j, 0)
    lsel = lambda b, j: (l, 0, 0)
    mod_map = (lambda b, j: (mod_base + b, 0, 0)) if per_batch_mod else (lambda b, j: (mod_base, 0, 0))
    return pl.pallas_call(
        _outproj_kernel,
        grid=(bsz, nt),
        in_specs=[
            pl.BlockSpec((tm, D_MODEL), row),
            pl.BlockSpec((None, N_MOD, D_MODEL), mod_map),
            pl.BlockSpec((tm, MLA_HEADS * MLA_V), row),
            pl.BlockSpec((tm, S5_CH), lambda b, j: (j, b)),
            pl.BlockSpec((tm, S5_CH), lambda b, j: (j, b)),
            pl.BlockSpec((tm, GQA_Q_HEADS * GQA_HEAD_DIM), row),
            pl.BlockSpec((None, S5_CH, S5_CH), lsel),
            pl.BlockSpec((None, 1, S5_CH), lsel),
            pl.BlockSpec((None, D_MODEL, D_MODEL), lsel, pipeline_mode=RESIDENT),
        ],
        out_specs=pl.BlockSpec((tm, D_MODEL), row),
        out_shape=jax.ShapeDtypeStruct((n, D_MODEL), F32),
        compiler_params=_cparams(("parallel", "parallel")),
        name="outproj",
    )(x, mods_l, o_mla, y_f, y_b, o_gqa, w["s5_w_glu"], w["s5_b_glu"], w["w_out"])


def _rope_tables(seq, identity):
    if identity:
        one = jnp.ones((seq, LANE), F32)
        zero = jnp.zeros((seq, LANE), F32)
        return one, zero, zero, one, zero
    n_rows = seq // GRID_W
    row = jnp.broadcast_to(jnp.arange(n_rows, dtype=F32)[:, None], (n_rows, GRID_W)).reshape(seq)
    col = jnp.broadcast_to(jnp.arange(GRID_W, dtype=F32)[None, :], (n_rows, GRID_W)).reshape(seq)

    def angles(rot_dim):
        n_freq = rot_dim // 4
        inv = ROPE_THETA ** (-jnp.arange(n_freq, dtype=F32) / n_freq)
        ang = jnp.concatenate([row[:, None] * inv, col[:, None] * inv], axis=-1)
        return jnp.cos(ang), jnp.sin(ang)

    cm, sm = angles(MLA_ROPE)
    zero32 = jnp.zeros_like(sm)
    pad = jnp.zeros((seq, LANE - MLA_ROPE), F32)
    cos_m = jnp.concatenate([cm, cm, jnp.ones_like(pad)], axis=-1)
    sa = jnp.concatenate([zero32, sm, pad], axis=-1)
    sb = jnp.concatenate([-sm, zero32, pad], axis=-1)
    cg, sg = angles(GQA_HEAD_DIM)
    cos_g = jnp.concatenate([cg, cg], axis=-1)
    sin_g = jnp.concatenate([-sg, sg], axis=-1)
    return cos_m, sa, sb, cos_g, sin_g


def kernel(x_prompt, x_sample, cache_mla_ckv, cache_mla_krope, cache_gqa_k, cache_gqa_v, state_s5, c, c_ctx, w_ada, b_ada, norm_ffn1, ffn1_w_gate, ffn1_w_up, ffn1_w_down, norm_mix, w_in, mla_q_norm, mla_w_uq, mla_kv_norm, mla_w_ukv, s5_lambda_re, s5_lambda_im, s5_log_dt, s5_b_re, s5_b_im, s5_c_re, s5_c_im, s5_d, s5_w_glu, s5_b_glu, gqa_q_norm, gqa_k_norm, w_out, norm_ffn2, ffn2_w_gate, ffn2_w_up, ffn2_w_down, norm_final):
    L = DEPTH
    cb, cs, _ = x_prompt.shape
    lb, ls, _ = x_sample.shape
    past = cache_mla_ckv.shape[2]

    zpad = jnp.zeros((L, D_MODEL, LANE - MLA_ROPE), F32)
    w_in_p = jnp.concatenate([w_in[:, :, :MLA_Q_RANK + MLA_KV_RANK + MLA_ROPE], zpad,
                              w_in[:, :, MLA_Q_RANK + MLA_KV_RANK + MLA_ROPE:]], axis=-1).astype(BF16)
    wuq = mla_w_uq.reshape(L, MLA_Q_RANK, MLA_HEADS, MLA_NOPE + MLA_ROPE)
    wuq = jnp.pad(wuq, ((0, 0), (0, 0), (0, 0), (0, MLA_QK_PAD - MLA_NOPE - MLA_ROPE)))
    wuq = wuq.reshape(L, MLA_Q_RANK, MLA_HEADS * MLA_QK_PAD).astype(BF16)
    wukv = mla_w_ukv.reshape(L, MLA_KV_RANK, MLA_HEADS, MLA_NOPE + MLA_V)
    wukv = jnp.concatenate([wukv[..., :MLA_NOPE].reshape(L, MLA_KV_RANK, MLA_HEADS * MLA_NOPE),
                            wukv[..., MLA_NOPE:].reshape(L, MLA_KV_RANK, MLA_HEADS * MLA_V)], axis=-1).astype(BF16)
    w = {
        "norm_mix": norm_mix.reshape(L, 1, D_MODEL),
        "w_in": w_in_p,
        "mla_q_norm": mla_q_norm.reshape(L, 1, MLA_Q_RANK),
        "mla_w_uq": wuq,
        "mla_kv_norm": mla_kv_norm.reshape(L, 1, MLA_KV_RANK),
        "mla_w_ukv": wukv,
        "gqa_q_norm": gqa_q_norm.reshape(L, 1, GQA_HEAD_DIM),
        "gqa_k_norm": gqa_k_norm.reshape(L, 1, GQA_HEAD_DIM),
        "s5_w_glu": s5_w_glu.astype(BF16),
        "s5_b_glu": s5_b_glu.reshape(L, 1, S5_CH),
        "w_out": w_out.astype(BF16),
    }
    f1 = (norm_ffn1.reshape(L, 1, D_MODEL), ffn1_w_gate.astype(BF16), ffn1_w_up.astype(BF16), ffn1_w_down.astype(BF16))
    f2 = (norm_ffn2.reshape(L, 1, D_MODEL), ffn2_w_gate.astype(BF16), ffn2_w_up.astype(BF16), ffn2_w_down.astype(BF16))
    bw, cw, ar, ai = _s5_weights(s5_lambda_re, s5_lambda_im, s5_log_dt, s5_b_re, s5_b_im, s5_c_re, s5_c_im)
    d_skip = s5_d.reshape(L, 1, S5_CH)
    rope_ctx = _rope_tables(256, True)
    rope_lat = _rope_tables(ls, False)
    cache_kr_pad = jnp.pad(cache_mla_krope, ((0, 0), (0, 0), (0, 0), (0, LANE - MLA_ROPE)))
    cache_k = cache_gqa_k.reshape(lb, L, past, GQA_KV_HEADS * GQA_HEAD_DIM)
    cache_vt = jnp.swapaxes(cache_gqa_v.reshape(lb, L, past, GQA_KV_HEADS * GQA_HEAD_DIM), 2, 3).astype(BF16)
    h0 = state_s5.reshape(lb, L, 2, S5_NSTATE, 2)
    zero_state = jnp.zeros((2, cb, S5_NSTATE), F32)

    cmat = jnp.concatenate([c_ctx[None, :], c, jnp.zeros((MOD_ROWS - 1 - lb, D_MODEL), F32)], axis=0)
    mods = _modulation(cmat, w_ada, b_ada)

    xp = x_prompt.reshape(cb * cs, D_MODEL)
    xs = x_sample.reshape(lb * ls, D_MODEL)
    st_ckv, st_kr, st_k, st_v, st_s5 = [], [], [], [], []

    def mixer(x, bsz, seq, mods_l, mod_base, per_batch_mod, l, rope, is_ctx):
        outs = _inproj(x, bsz, seq, mods_l, mod_base, per_batch_mod, w, l, rope, is_ctx)
        qm, km, vm, u_tm, gq, gk, gv = outs[:7]
        grep = GQA_Q_HEADS // GQA_KV_HEADS
        mla_hps, gqa_hps = (MLA_HEADS, GQA_Q_HEADS) if is_ctx else (ATTN_LAT_HPS, ATTN_LAT_HPS)
        mla_parts = [_kv_part_2d(km, vm, seq, MLA_QK_PAD, MLA_V, MLA_HEADS, mla_hps)]
        gqa_parts = [_kv_part_2d(gk, gv, seq, GQA_HEAD_DIM, GQA_HEAD_DIM, GQA_KV_HEADS, gqa_hps // grep)]
        if is_ctx:
            h0r = h0i = zero_state
            tq, tt = 256, 32
        else:
            km_c, vm_c = _kvup(cache_mla_ckv, cache_kr_pad, w["mla_w_ukv"], l)
            mla_parts = [_kv_part_2d(km_c, vm_c, past, MLA_QK_PAD, MLA_V, MLA_HEADS, mla_hps)] + mla_parts
            gqa_parts = [_kv_part_cache(cache_k, cache_vt, past, GQA_HEAD_DIM, GQA_HEAD_DIM, gqa_hps // grep, l)] + gqa_parts
            h0r = jnp.swapaxes(h0[:, l, :, :, 0], 0, 1)
            h0i = jnp.swapaxes(h0[:, l, :, :, 1], 0, 1)
            tq, tt = 1024, 64
        o_mla = _attention(qm, mla_parts, bsz, seq, MLA_HEADS, 1, MLA_QK_PAD, MLA_V, tq, mla_hps)
        o_gqa = _attention(gq, gqa_parts, bsz, seq, GQA_Q_HEADS, grep, GQA_HEAD_DIM, GQA_HEAD_DIM, tq, gqa_hps)
        y_f, y_b, fin_r, fin_i = _s5_scan(u_tm, d_skip, bw, cw, ar, ai, h0r, h0i, bsz, seq, tt, l)
        x = _outproj(x, bsz, seq, mods_l, mod_base, per_batch_mod, o_mla, y_f, y_b, o_gqa, w, l)
        extra = None
        if is_ctx:
            ckv, kr, kg, vg = outs[7:]
            s5s = jnp.swapaxes(jnp.stack([fin_r, fin_i], axis=-1), 0, 1)
            extra = (ckv.reshape(bsz, seq, MLA_KV_RANK), kr.reshape(bsz, seq, MLA_ROPE),
                     kg.reshape(bsz, seq, GQA_KV_HEADS, GQA_HEAD_DIM), vg.reshape(bsz, seq, GQA_KV_HEADS, GQA_HEAD_DIM),
                     s5s.reshape(bsz, 2, S5_GROUPS, S5_STATE, 2))
        return x, extra

    g_final = norm_final.reshape(1, D_MODEL)
    for l in range(L):
        mods_l = mods[l]
        fin = g_final if l == L - 1 else None
        xp = _ffn(xp, mods_l, 0, cb * cs, *f1, l, 0)
        xp, (ckv, kr, kg, vg, s5s) = mixer(xp, cb, cs, mods_l, 0, False, l, rope_ctx, True)
        xp = _ffn(xp, mods_l, 0, cb * cs, *f2, l, 6, fin)
        st_ckv.append(ckv)
        st_kr.append(kr)
        st_k.append(kg)
        st_v.append(vg)
        st_s5.append(s5s)
        xs = _ffn(xs, mods_l, 1, ls, *f1, l, 0)
        xs, _ = mixer(xs, lb, ls, mods_l, 1, True, l, rope_lat, False)
        xs = _ffn(xs, mods_l, 1, ls, *f2, l, 6, fin)

    y_prompt = xp.reshape(cb, cs, D_MODEL)
    y_sample = xs.reshape(lb, ls, D_MODEL)
    return (y_prompt, y_sample, jnp.stack(st_ckv, axis=1), jnp.stack(st_kr, axis=1), jnp.stack(st_k, axis=1),
            jnp.stack(st_v, axis=1), jnp.stack(st_s5, axis=1))
```

```python
import functools
import math

import jax
import jax.numpy as jnp
from jax import lax
from jax.experimental import pallas as pl
from jax.experimental.pallas import tpu as pltpu

D_MODEL = 2048
DEPTH = 4
GRID_W = 64
ROPE_THETA = 10000.0
EPS = 1e-6
N_MOD = 9
D_FF = 2 * D_MODEL

MLA_HEADS = 8
MLA_Q_RANK = D_MODEL // 4
MLA_KV_RANK = D_MODEL // 8
MLA_NOPE = 128
MLA_ROPE = 64
MLA_V = 128
MLA_QK_PAD = 256

S5_CH = D_MODEL // 4
S5_GROUP = 16
S5_GROUPS = S5_CH // S5_GROUP
S5_STATE = 64
S5_NSTATE = S5_GROUPS * S5_STATE

GQA_Q_HEADS = 4
GQA_KV_HEADS = 2
GQA_HEAD_DIM = 128

MLA_SCALE = 1.0 / math.sqrt(MLA_NOPE + MLA_ROPE)
GQA_SCALE = 1.0 / math.sqrt(GQA_HEAD_DIM)
LOG2E = math.log2(math.e)
MLA_QSCALE = MLA_SCALE * LOG2E
GQA_QSCALE = GQA_SCALE * LOG2E

ZQ0, ZKV0, ZKR0, ZU0, ZGQ0, ZGK0, ZGV0, Z_END = 0, 512, 768, 896, 1408, 1920, 2176, 2432

LANE = 128
SUBLANE = 8
MOD_ROWS = 16

F32 = jnp.float32
BF16 = jnp.bfloat16

VMEM_LIMIT = 56 * 1024 * 1024
INPROJ_TM = 256
OUTPROJ_TM = 512
RESIDENT = pl.Buffered(1)


def _cparams(sem, vmem_limit=VMEM_LIMIT):
    return pltpu.CompilerParams(dimension_semantics=sem, vmem_limit_bytes=vmem_limit)


def _rms(x, g):
    return x * lax.rsqrt(jnp.mean(x * x, axis=-1, keepdims=True) + EPS) * g


def _sigmoid(x):
    return 1.0 / (1.0 + jnp.exp(-x))


def _dot(a, b):
    return jnp.dot(a, b, preferred_element_type=F32)


def _mod_kernel(c_ref, w_ref, b_ref, o_ref):
    c = c_ref[...]
    a = (c * _sigmoid(c)).astype(BF16)
    o_ref[...] = _dot(a, w_ref[...].astype(BF16)) + b_ref[...]


def _modulation(cmat, w_ada, b_ada):
    tn = 1024
    nd = N_MOD * D_MODEL
    out = pl.pallas_call(
        _mod_kernel,
        grid=(DEPTH, nd // tn),
        in_specs=[
            pl.BlockSpec((MOD_ROWS, D_MODEL), lambda l, j: (0, 0)),
            pl.BlockSpec((None, D_MODEL, tn), lambda l, j: (l, 0, j)),
            pl.BlockSpec((None, 1, tn), lambda l, j: (l, 0, j)),
        ],
        out_specs=pl.BlockSpec((None, MOD_ROWS, tn), lambda l, j: (l, 0, j)),
        out_shape=jax.ShapeDtypeStruct((DEPTH, MOD_ROWS, nd), F32),
        compiler_params=_cparams(("parallel", "parallel")),
        name="modulation",
    )(cmat, w_ada, b_ada.reshape(DEPTH, 1, nd))
    return out.reshape(DEPTH, MOD_ROWS, N_MOD, D_MODEL)


def _ffn_kernel(x_ref, xn_ref, mod_ref, modn_ref, g_ref, wg_hbm, wu_hbm, wd_hbm, *rest, k0, final, l):
    if final:
        gf_ref, o_ref, wg_buf, wu_buf, wd_buf, hm_ref, sem = rest
    else:
        o_ref, wg_buf, wu_buf, wd_buf, hm_ref, sem = rest
    i = pl.program_id(0)
    nff = D_FF // FFN_TF
    cur = lax.rem(i, 2)

    def modulated(x, m_ref):
        return (_rms(x, g_ref[...]) * (1.0 + m_ref[k0 + 1:k0 + 2, :]) + m_ref[k0:k0 + 1, :]).astype(BF16)

    def tile_copies(j, slot):
        cols = pl.ds(j * FFN_TF, FFN_TF)
        return (pltpu.make_async_copy(wg_hbm.at[l, :, cols], wg_buf.at[slot], sem.at[0, slot]),
                pltpu.make_async_copy(wu_hbm.at[l, :, cols], wu_buf.at[slot], sem.at[1, slot]),
                pltpu.make_async_copy(wd_hbm.at[l, cols, :], wd_buf.at[slot], sem.at[2, slot]))

    @pl.when(i == 0)
    def _():
        for c in tile_copies(0, 0):
            c.start()
        hm_ref[0] = modulated(x_ref[...], mod_ref)

    for j in range(nff):
        slot = j % 2
        if j + 1 < nff:
            for c in tile_copies(j + 1, 1 - slot):
                c.start()
        else:
            @pl.when(i + 1 < pl.num_programs(0))
            def _():
                for c in tile_copies(0, 1 - slot):
                    c.start()
        for c in tile_copies(j, slot):
            c.wait()
        for c0 in range(0, FFN_TF, FFN_SUB):
            h = hm_ref[cur]
            g = _dot(h, wg_buf[slot, :, c0:c0 + FFN_SUB])
            u = _dot(h, wu_buf[slot, :, c0:c0 + FFN_SUB])
            a = (g * _sigmoid(g)) * u
            d = _dot(a.astype(BF16), wd_buf[slot, c0:c0 + FFN_SUB, :])
            if j == 0 and c0 == 0:
                o_ref[...] = d
            else:
                o_ref[...] += d
    out = x_ref[...] + (0.5 * mod_ref[k0 + 2:k0 + 3, :]) * o_ref[...]
    if final:
        out = _rms(out, gf_ref[...])
    o_ref[...] = out
    hm_ref[1 - cur] = modulated(xn_ref[...], modn_ref)


FFN_VMEM_LIMIT = 62 * 1024 * 1024
FFN_TM = 512
FFN_TF = 1024
FFN_SUB = 512
assert (D_FF // FFN_TF) % 2 == 0


def _ffn(x, mods_l, mod_base, rows_per_mod, norm_g, wg, wu, wd, l, k0, final_g=None):
    n = x.shape[0]
    tm = FFN_TM
    nrt = n // tm
    nxt = lambda i: jnp.minimum(i + 1, nrt - 1)
    in_specs = [
        pl.BlockSpec((tm, D_MODEL), lambda i: (i, 0)),
        pl.BlockSpec((tm, D_MODEL), lambda i: (nxt(i), 0)),
        pl.BlockSpec((None, N_MOD, D_MODEL), lambda i: (mod_base + (i * tm) // rows_per_mod, 0, 0)),
        pl.BlockSpec((None, N_MOD, D_MODEL), lambda i: (mod_base + (nxt(i) * tm) // rows_per_mod, 0, 0)),
        pl.BlockSpec((None, 1, D_MODEL), lambda i: (l, 0, 0)),
        pl.BlockSpec(memory_space=pl.ANY),
        pl.BlockSpec(memory_space=pl.ANY),
        pl.BlockSpec(memory_space=pl.ANY),
    ]
    args = [x, x, mods_l, mods_l, norm_g, wg, wu, wd]
    if final_g is not None:
        in_specs.append(pl.BlockSpec((1, D_MODEL), lambda i: (0, 0)))
        args.append(final_g)
    return pl.pallas_call(
        functools.partial(_ffn_kernel, k0=k0, final=final_g is not None, l=l),
        grid=(n // tm,),
        in_specs=in_specs,
        out_specs=pl.BlockSpec((tm, D_MODEL), lambda i: (i, 0)),
        out_shape=jax.ShapeDtypeStruct((n, D_MODEL), F32),
        scratch_shapes=[
            pltpu.VMEM((2, D_MODEL, FFN_TF), BF16),
            pltpu.VMEM((2, D_MODEL, FFN_TF), BF16),
            pltpu.VMEM((2, FFN_TF, D_MODEL), BF16),
            pltpu.VMEM((2, tm, D_MODEL), BF16),
            pltpu.SemaphoreType.DMA((3, 2)),
        ],
        compiler_params=_cparams(("arbitrary",), FFN_VMEM_LIMIT),
        name="ffn",
    )(*args)


def _rope_mla(t, cos, sa, sb):
    return t * cos + pltpu.roll(t, 32, 1) * sa + pltpu.roll(t, 96, 1) * sb


def _rope_gqa(t, cos, ss):
    return t * cos + pltpu.roll(t, 64, 1) * ss


def _inproj_kernel(x_ref, mod_ref, g_ref, win_ref, qn_ref, wuq_ref, kvn_ref, wukv_ref, gqn_ref, gkn_ref,
                   cm_ref, sa_ref, sb_ref, cg_ref, sg_ref,
                   qm_ref, km_ref, vm_ref, u_ref, gq_ref, gk_ref, gv_ref, *cache_refs):
    x = x_ref[...]
    h = (_rms(x, g_ref[...]) * (1.0 + mod_ref[4:5, :]) + mod_ref[3:4, :]).astype(BF16)
    z = _dot(h, win_ref[...])
    cm, sa, sb = cm_ref[...], sa_ref[...], sb_ref[...]
    cg, sg = cg_ref[...], sg_ref[...]

    ckv = _rms(z[:, ZKV0:ZKR0], kvn_ref[...])
    kr = _rope_mla(z[:, ZKR0:ZU0], cm, sa, sb)
    kr_b = kr.astype(BF16)

    q = _dot(_rms(z[:, ZQ0:ZKV0], qn_ref[...]).astype(BF16), wuq_ref[...])
    kv = _dot(ckv.astype(BF16), wukv_ref[...])
    for hd in range(MLA_HEADS):
        c0 = hd * MLA_QK_PAD
        qm_ref[:, c0:c0 + LANE] = (q[:, c0:c0 + LANE] * MLA_QSCALE).astype(BF16)
        qm_ref[:, c0 + LANE:c0 + 2 * LANE] = (
            _rope_mla(q[:, c0 + LANE:c0 + 2 * LANE], cm, sa, sb) * MLA_QSCALE).astype(BF16)
        km_ref[:, c0:c0 + LANE] = kv[:, hd * LANE:(hd + 1) * LANE].astype(BF16)
        km_ref[:, c0 + LANE:c0 + 2 * LANE] = kr_b
    vm_ref[...] = kv[:, MLA_HEADS * MLA_NOPE:].T.astype(BF16)

    u_ref[...] = z[:, ZU0:ZGQ0]

    for hd in range(GQA_Q_HEADS):
        t = _rms(z[:, ZGQ0 + hd * LANE:ZGQ0 + (hd + 1) * LANE], gqn_ref[...])
        gq_ref[:, hd * LANE:(hd + 1) * LANE] = (_rope_gqa(t, cg, sg) * GQA_QSCALE).astype(BF16)
    kgs = []
    for hd in range(GQA_KV_HEADS):
        t = _rms(z[:, ZGK0 + hd * LANE:ZGK0 + (hd + 1) * LANE], gkn_ref[...])
        t = _rope_gqa(t, cg, sg)
        kgs.append(t)
        gk_ref[:, hd * LANE:(hd + 1) * LANE] = t.astype(BF16)
    gv = z[:, ZGV0:Z_END]
    gv_ref[...] = gv.T.astype(BF16)

    if cache_refs:
        ckv_o, kr_o, kg_o, vg_o = cache_refs
        ckv_o[...] = ckv
        kr_o[...] = kr[:, :MLA_ROPE]
        for hd in range(GQA_KV_HEADS):
            kg_o[:, hd * LANE:(hd + 1) * LANE] = kgs[hd]
        vg_o[...] = gv


def _inproj(x, bsz, seq, mods_l, mod_base, per_batch_mod, w, l, rope, emit_cache):
    n = bsz * seq
    tm = min(INPROJ_TM, seq)
    nt = seq // tm
    cm, sa, sb, cg, sg = rope
    rope_blocks = cm.shape[0] // tm
    rmap = (lambda b, j: (j, 0)) if rope_blocks > 1 else (lambda b, j: (0, 0))
    row = lambda b, j: (b * nt + j, 0)
    lsel = lambda b, j: (l, 0, 0)
    mod_map = (lambda b, j: (mod_base + b, 0, 0)) if per_batch_mod else (lambda b, j: (mod_base, 0, 0))
    out_shape = [
        jax.ShapeDtypeStruct((n, MLA_HEADS * MLA_QK_PAD), BF16),
        jax.ShapeDtypeStruct((n, MLA_HEADS * MLA_QK_PAD), BF16),
        jax.ShapeDtypeStruct((bsz * MLA_HEADS * MLA_V, seq), BF16),
        jax.ShapeDtypeStruct((seq, bsz * S5_CH), F32),
        jax.ShapeDtypeStruct((n, GQA_Q_HEADS * GQA_HEAD_DIM), BF16),
        jax.ShapeDtypeStruct((n, GQA_KV_HEADS * GQA_HEAD_DIM), BF16),
        jax.ShapeDtypeStruct((bsz * GQA_KV_HEADS * GQA_HEAD_DIM, seq), BF16),
    ]
    out_specs = [
        pl.BlockSpec((tm, MLA_HEADS * MLA_QK_PAD), row),
        pl.BlockSpec((tm, MLA_HEADS * MLA_QK_PAD), row),
        pl.BlockSpec((MLA_HEADS * MLA_V, tm), lambda b, j: (b, j)),
        pl.BlockSpec((tm, S5_CH), lambda b, j: (j, b)),
        pl.BlockSpec((tm, GQA_Q_HEADS * GQA_HEAD_DIM), row),
        pl.BlockSpec((tm, GQA_KV_HEADS * GQA_HEAD_DIM), row),
        pl.BlockSpec((GQA_KV_HEADS * GQA_HEAD_DIM, tm), lambda b, j: (b, j)),
    ]
    if emit_cache:
        out_shape += [
            jax.ShapeDtypeStruct((n, MLA_KV_RANK), F32),
            jax.ShapeDtypeStruct((n, MLA_ROPE), F32),
            jax.ShapeDtypeStruct((n, GQA_KV_HEADS * GQA_HEAD_DIM), F32),
            jax.ShapeDtypeStruct((n, GQA_KV_HEADS * GQA_HEAD_DIM), F32),
        ]
        out_specs += [
            pl.BlockSpec((tm, MLA_KV_RANK), row),
            pl.BlockSpec((tm, MLA_ROPE), row),
            pl.BlockSpec((tm, GQA_KV_HEADS * GQA_HEAD_DIM), row),
            pl.BlockSpec((tm, GQA_KV_HEADS * GQA_HEAD_DIM), row),
        ]
    return pl.pallas_call(
        _inproj_kernel,
        grid=(bsz, nt),
        in_specs=[
            pl.BlockSpec((tm, D_MODEL), row),
            pl.BlockSpec((None, N_MOD, D_MODEL), mod_map),
            pl.BlockSpec((None, 1, D_MODEL), lsel),
            pl.BlockSpec((None, D_MODEL, Z_END), lsel, pipeline_mode=RESIDENT),
            pl.BlockSpec((None, 1, MLA_Q_RANK), lsel),
            pl.BlockSpec((None, MLA_Q_RANK, MLA_HEADS * MLA_QK_PAD), lsel, pipeline_mode=RESIDENT),
            pl.BlockSpec((None, 1, MLA_KV_RANK), lsel),
            pl.BlockSpec((None, MLA_KV_RANK, MLA_HEADS * (MLA_NOPE + MLA_V)), lsel, pipeline_mode=RESIDENT),
            pl.BlockSpec((None, 1, GQA_HEAD_DIM), lsel),
            pl.BlockSpec((None, 1, GQA_HEAD_DIM), lsel),
        ] + [pl.BlockSpec((tm, LANE), rmap)] * 5,
        out_specs=out_specs,
        out_shape=out_shape,
        compiler_params=_cparams(("parallel", "parallel")),
        name="inproj",
    )(x, mods_l, w["norm_mix"], w["w_in"], w["mla_q_norm"], w["mla_w_uq"], w["mla_kv_norm"], w["mla_w_ukv"],
      w["gqa_q_norm"], w["gqa_k_norm"], cm, sa, sb, cg, sg)


def _kvup_kernel(ckv_ref, kr_ref, wukv_ref, km_ref, vm_ref):
    kv = _dot(ckv_ref[...].astype(BF16), wukv_ref[...])
    kr_b = kr_ref[...].astype(BF16)
    for hd in range(MLA_HEADS):
        c0 = hd * MLA_QK_PAD
        km_ref[:, c0:c0 + LANE] = kv[:, hd * LANE:(hd + 1) * LANE].astype(BF16)
        km_ref[:, c0 + LANE:c0 + 2 * LANE] = kr_b
    vm_ref[...] = kv[:, MLA_HEADS * MLA_NOPE:].T.astype(BF16)


def _kvup(cache_ckv, cache_kr_pad, wukv, l):
    bsz, _, past, _ = cache_ckv.shape
    n = bsz * past
    return pl.pallas_call(
        _kvup_kernel,
        grid=(bsz,),
        in_specs=[
            pl.BlockSpec((None, None, past, MLA_KV_RANK), lambda b: (b, l, 0, 0)),
            pl.BlockSpec((None, None, past, LANE), lambda b: (b, l, 0, 0)),
            pl.BlockSpec((None, MLA_KV_RANK, MLA_HEADS * (MLA_NOPE + MLA_V)), lambda b: (l, 0, 0)),
        ],
        out_specs=[
            pl.BlockSpec((past, MLA_HEADS * MLA_QK_PAD), lambda b: (b, 0)),
            pl.BlockSpec((MLA_HEADS * MLA_V, past), lambda b: (b, 0)),
        ],
        out_shape=[
            jax.ShapeDtypeStruct((n, MLA_HEADS * MLA_QK_PAD), BF16),
            jax.ShapeDtypeStruct((bsz * MLA_HEADS * MLA_V, past), BF16),
        ],
        compiler_params=_cparams(("parallel",)),
        name="kvup",
    )(cache_ckv, cache_kr_pad, wukv)


ATTN_CK = 256
ATTN_QK_ROWS = 2048
ATTN_SUB = 512
ATTN_LAT_HPS = 2


def _attn_kernel(*refs, nparts, hps, rep, dk, dv):
    q_ref = refs[0]
    k_refs = refs[1:1 + nparts]
    vt_refs = refs[1 + nparts:1 + 2 * nparts]
    o_ref = refs[1 + 2 * nparts]
    s_ref = refs[2 + 2 * nparts]
    tq = q_ref.shape[0]
    ts = min(tq, ATTN_SUB)
    dn = (((1,), (1,)), ((), ()))
    def key_chunks(size):
        out, off = [], 0
        for pi, k_ref in enumerate(k_refs):
            rows = k_ref.shape[0]
            for r0 in range(0, rows, size):
                ck = min(size, rows - r0)
                out.append((pi, r0, ck, off))
                off += ck
        return out

    chunks = key_chunks(ATTN_CK)
    items = [(hh, q0) for hh in range(hps) for q0 in range(0, tq, ts)]

    maxes = []
    for hh, q0 in items:
        kv = hh // rep
        q = q_ref[q0:q0 + ts, hh * dk:(hh + 1) * dk]
        mpart = None
        for pi, r0, ck, off in key_chunks(ATTN_QK_ROWS):
            k = k_refs[pi][r0:r0 + ck, kv * dk:(kv + 1) * dk].astype(BF16)
            s = lax.dot_general(k, q, dn, preferred_element_type=F32)
            s_ref[hh, off:off + ck, q0:q0 + ts] = s
            blk = jnp.max(s.reshape(ck // SUBLANE, SUBLANE, ts), axis=0)
            mpart = blk if mpart is None else jnp.maximum(mpart, blk)
        maxes.append(jnp.max(mpart, axis=0, keepdims=True))

    for (hh, q0), m in zip(items, maxes):
        kv = hh // rep
        lpart = jnp.zeros((SUBLANE, ts), F32)
        acc = jnp.zeros((dv, ts), F32)
        for pi, r0, ck, off in chunks:
            p = jnp.exp2(s_ref[hh, off:off + ck, q0:q0 + ts] - m)
            lpart = lpart + jnp.sum(p.reshape(ck // SUBLANE, SUBLANE, ts), axis=0)
            vt = vt_refs[pi][kv * dv:(kv + 1) * dv, r0:r0 + ck].astype(BF16)
            acc = acc + _dot(vt, p.astype(BF16))
        den = jnp.sum(lpart, axis=0, keepdims=True)
        o_ref[q0:q0 + ts, hh * dv:(hh + 1) * dv] = (acc / den).T.astype(o_ref.dtype)


def _attention(q, parts, bsz, seq, heads, rep, dk, dv, tq, hps):
    nq = seq // tq
    n_keys = sum(p[4] for p in parts)
    assert all(p[4] % LANE == 0 for p in parts) and heads % hps == 0 and hps % rep == 0
    in_specs = [pl.BlockSpec((tq, hps * dk), lambda b, h, i: (b * nq + i, h))]
    in_specs += [p[2] for p in parts] + [p[3] for p in parts]
    args = [q] + [p[0] for p in parts] + [p[1] for p in parts]
    return pl.pallas_call(
        functools.partial(_attn_kernel, nparts=len(parts), hps=hps, rep=rep, dk=dk, dv=dv),
        grid=(bsz, heads // hps, nq),
        in_specs=in_specs,
        out_specs=pl.BlockSpec((tq, hps * dv), lambda b, h, i: (b * nq + i, h)),
        out_shape=jax.ShapeDtypeStruct((bsz * seq, heads * dv), BF16),
        scratch_shapes=[pltpu.VMEM((hps, n_keys, tq), F32)],
        compiler_params=_cparams(("parallel", "parallel", "arbitrary")),
        name="attention",
    )(*args)


def _kv_part_2d(k, vt, rows, dk, dv, kv_heads, kps):
    ng = kv_heads // kps
    return (k, vt,
            pl.BlockSpec((rows, kps * dk), lambda b, h, i: (b, h)),
            pl.BlockSpec((kps * dv, rows), lambda b, h, i: (b * ng + h, 0)), rows)


def _kv_part_cache(k, vt, rows, dk, dv, kps, l):
    return (k, vt,
            pl.BlockSpec((None, None, rows, kps * dk), lambda b, h, i: (b, l, 0, h)),
            pl.BlockSpec((None, None, kps * dv, rows), lambda b, h, i: (b, l, h, 0)), rows)


S5_LC = 512
S5_Q = S5_CH // LANE
S5_QS = S5_NSTATE // S5_Q


def _s5_kernel(uf_ref, ub_ref, d_ref, bw_ref, cw_ref, ar_ref, ai_ref, h0r_ref, h0i_ref,
               yf_ref, yb_ref, fr_ref, fi_ref, ut_ref, yt_ref, hr_ref, hi_ref, sr_ref, si_ref, *, bsz, tt):
    j = pl.program_id(0)

    @pl.when(j == 0)
    def _():
        sr_ref[...] = h0r_ref[...]
        si_ref[...] = h0i_ref[...]

    for d, (u_ref, y_ref) in enumerate(((uf_ref, yf_ref), (ub_ref, yb_ref))):
        for b in range(bsz):
            for q in range(S5_Q):
                c0 = b * S5_CH + q * LANE
                ut_ref[d, q, pl.ds(b, tt, stride=bsz), :] = u_ref[:, c0:c0 + LANE]
        for q in range(S5_Q):
            bu = _dot(ut_ref[d, q].astype(BF16), bw_ref[d, q])
            hr_ref[d, :, q * S5_QS:(q + 1) * S5_QS] = bu[:, :S5_QS]
            hi_ref[d, :, q * S5_QS:(q + 1) * S5_QS] = bu[:, S5_QS:]

        for c in range(S5_NSTATE // S5_LC):
            lanes = slice(c * S5_LC, (c + 1) * S5_LC)
            ar = jnp.broadcast_to(ar_ref[d, :, lanes], (bsz, S5_LC))
            ai = jnp.broadcast_to(ai_ref[d, :, lanes], (bsz, S5_LC))
            pr = sr_ref[d, :, lanes]
            pi = si_ref[d, :, lanes]
            for t in (range(tt - 1, -1, -1) if d == 1 else range(tt)):
                rows = slice(t * bsz, (t + 1) * bsz)
                nr = ar * pr - ai * pi + hr_ref[d, rows, lanes]
                ni = ar * pi + ai * pr + hi_ref[d, rows, lanes]
                hr_ref[d, rows, lanes] = nr
                hi_ref[d, rows, lanes] = ni
                pr, pi = nr, ni
            sr_ref[d, :, lanes] = pr
            si_ref[d, :, lanes] = pi

        for q in range(S5_Q):
            hs = slice(q * S5_QS, (q + 1) * S5_QS)
            cols = slice(q * LANE, (q + 1) * LANE)
            yq = (_dot(hr_ref[d, :, hs].astype(BF16), cw_ref[d, q, :S5_QS, :])
                  + _dot(hi_ref[d, :, hs].astype(BF16), cw_ref[d, q, S5_QS:, :]))
            if d == 0:
                yq = d_ref[:, cols] * ut_ref[d, q] + yq
            yt_ref[d, q] = yq
        for b in range(bsz):
            for q in range(S5_Q):
                c0 = b * S5_CH + q * LANE
                y_ref[:, c0:c0 + LANE] = yt_ref[d, q, pl.ds(b, tt, stride=bsz), :]

    fr_ref[...] = sr_ref[...]
    fi_ref[...] = si_ref[...]


def _s5_scan(u, d_skip, bw, cw, ar, ai, h0r, h0i, bsz, seq, tt, l):
    rows = tt * bsz
    nt = seq // tt
    fmap = lambda j: (j, 0)
    bmap = lambda j: (nt - 1 - j, 0)
    wsel = lambda j: (l, 0, 0, 0, 0)
    asel = lambda j: (l, 0, 0, 0)
    ssel = lambda j: (0, 0, 0)
    return pl.pallas_call(
        functools.partial(_s5_kernel, bsz=bsz, tt=tt),
        grid=(nt,),
        in_specs=[
            pl.BlockSpec((tt, bsz * S5_CH), fmap),
            pl.BlockSpec((tt, bsz * S5_CH), bmap),
            pl.BlockSpec((None, 1, S5_CH), lambda j: (l, 0, 0)),
            pl.BlockSpec((None, 2, S5_Q, LANE, 2 * S5_QS), wsel),
            pl.BlockSpec((None, 2, S5_Q, 2 * S5_QS, LANE), wsel),
            pl.BlockSpec((None, 2, 1, S5_NSTATE), asel),
            pl.BlockSpec((None, 2, 1, S5_NSTATE), asel),
            pl.BlockSpec((2, bsz, S5_NSTATE), ssel),
            pl.BlockSpec((2, bsz, S5_NSTATE), ssel),
        ],
        out_specs=[
            pl.BlockSpec((tt, bsz * S5_CH), fmap),
            pl.BlockSpec((tt, bsz * S5_CH), bmap),
            pl.BlockSpec((2, bsz, S5_NSTATE), ssel),
            pl.BlockSpec((2, bsz, S5_NSTATE), ssel),
        ],
        out_shape=[
            jax.ShapeDtypeStruct((seq, bsz * S5_CH), F32),
            jax.ShapeDtypeStruct((seq, bsz * S5_CH), F32),
            jax.ShapeDtypeStruct((2, bsz, S5_NSTATE), F32),
            jax.ShapeDtypeStruct((2, bsz, S5_NSTATE), F32),
        ],
        scratch_shapes=[
            pltpu.VMEM((2, S5_Q, rows, LANE), F32),
            pltpu.VMEM((2, S5_Q, rows, LANE), F32),
            pltpu.VMEM((2, rows, S5_NSTATE), F32),
            pltpu.VMEM((2, rows, S5_NSTATE), F32),
            pltpu.VMEM((2, bsz, S5_NSTATE), F32),
            pltpu.VMEM((2, bsz, S5_NSTATE), F32),
        ],
        compiler_params=_cparams(("arbitrary",)),
        name="s5_scan",
    )(u, u, d_skip, bw, cw, ar, ai, h0r, h0i)


def _s5_weights(lam_re, lam_im, log_dt, b_re, b_im, c_re, c_im):
    dt = jnp.exp(log_dt)[..., None]
    mag = jnp.exp(lam_re * dt)
    ab_re = mag * jnp.cos(lam_im * dt)
    ab_im = mag * jnp.sin(lam_im * dt)
    den = lam_re * lam_re + lam_im * lam_im
    nr = ab_re - 1.0
    f_re = (nr * lam_re + ab_im * lam_im) / den
    f_im = (ab_im * lam_re - nr * lam_im) / den
    bb_re = f_re[..., None] * b_re - f_im[..., None] * b_im
    bb_im = f_re[..., None] * b_im + f_im[..., None] * b_re
    eye = jnp.eye(8, dtype=F32)
    lshape = lam_re.shape[:2]

    def pack_b(bb):
        t = bb.reshape(*lshape, S5_Q, 8, S5_STATE, S5_GROUP)
        return jnp.einsum("ldqrpc,rs->ldqrcsp", t, eye).reshape(*lshape, S5_Q, LANE, S5_QS)

    def pack_c(cc):
        t = cc.reshape(*lshape, S5_Q, 8, S5_GROUP, S5_STATE)
        return jnp.einsum("ldqrcp,rs->ldqrpsc", t, eye).reshape(*lshape, S5_Q, S5_QS, LANE)

    bw = jnp.concatenate([pack_b(bb_re), pack_b(bb_im)], axis=-1).astype(BF16)
    cw = jnp.concatenate([pack_c(c_re), -pack_c(c_im)], axis=-2).astype(BF16)
    ar = ab_re.reshape(*lshape, 1, S5_NSTATE)
    ai = ab_im.reshape(*lshape, 1, S5_NSTATE)
    return bw, cw, ar, ai


def _outproj_kernel(x_ref, mod_ref, om_ref, yf_ref, yb_ref, og_ref, wglu_ref, bglu_ref, wout_ref, o_ref):
    y = yf_ref[...] + yb_ref[...]
    gate = _sigmoid(_dot(y.astype(BF16), wglu_ref[...]) + bglu_ref[...])
    gelu = 0.5 * y * (1.0 + jnp.tanh(math.sqrt(2.0 / math.pi) * (y + 0.044715 * (y * y * y))))
    s5 = (gelu * gate).astype(BF16)
    n_mla = MLA_HEADS * MLA_V
    mixed = (_dot(om_ref[...], wout_ref[0:n_mla, :])
             + _dot(s5, wout_ref[n_mla:n_mla + S5_CH, :])
             + _dot(og_ref[...], wout_ref[n_mla + S5_CH:, :]))
    o_ref[...] = x_ref[...] + mod_ref[5:6, :] * mixed


def _outproj(x, bsz, seq, mods_l, mod_base, per_batch_mod, o_mla, y_f, y_b, o_gqa, w, l):
    n = bsz * seq
    tm = min(OUTPROJ_TM, seq)
    nt = seq // tm
    row = lambda b, j: (b * nt + j, 0)
    lsel = lambda b, j: (l, 0, 0)
    mod_map = (lambda b, j: (mod_base + b, 0, 0)) if per_batch_mod else (lambda b, j: (mod_base, 0, 0))
    return pl.pallas_call(
        _outproj_kernel,
        grid=(bsz, nt),
        in_specs=[
            pl.BlockSpec((tm, D_MODEL), row),
            pl.BlockSpec((None, N_MOD, D_MODEL), mod_map),
            pl.BlockSpec((tm, MLA_HEADS * MLA_V), row),
            pl.BlockSpec((tm, S5_CH), lambda b, j: (j, b)),
            pl.BlockSpec((tm, S5_CH), lambda b, j: (j, b)),
            pl.BlockSpec((tm, GQA_Q_HEADS * GQA_HEAD_DIM), row),
            pl.BlockSpec((None, S5_CH, S5_CH), lsel),
            pl.BlockSpec((None, 1, S5_CH), lsel),
            pl.BlockSpec((None, D_MODEL, D_MODEL), lsel, pipeline_mode=RESIDENT),
        ],
        out_specs=pl.BlockSpec((tm, D_MODEL), row),
        out_shape=jax.ShapeDtypeStruct((n, D_MODEL), F32),
        compiler_params=_cparams(("parallel", "parallel")),
        name="outproj",
    )(x, mods_l, o_mla, y_f, y_b, o_gqa, w["s5_w_glu"], w["s5_b_glu"], w["w_out"])


def _rope_tables(seq, identity):
    if identity:
        one = jnp.ones((seq, LANE), F32)
        zero = jnp.zeros((seq, LANE), F32)
        return one, zero, zero, one, zero
    n_rows = seq // GRID_W
    row = jnp.broadcast_to(jnp.arange(n_rows, dtype=F32)[:, None], (n_rows, GRID_W)).reshape(seq)
    col = jnp.broadcast_to(jnp.arange(GRID_W, dtype=F32)[None, :], (n_rows, GRID_W)).reshape(seq)

    def angles(rot_dim):
        n_freq = rot_dim // 4
        inv = ROPE_THETA ** (-jnp.arange(n_freq, dtype=F32) / n_freq)
        ang = jnp.concatenate([row[:, None] * inv, col[:, None] * inv], axis=-1)
        return jnp.cos(ang), jnp.sin(ang)

    cm, sm = angles(MLA_ROPE)
    zero32 = jnp.zeros_like(sm)
    pad = jnp.zeros((seq, LANE - MLA_ROPE), F32)
    cos_m = jnp.concatenate([cm, cm, jnp.ones_like(pad)], axis=-1)
    sa = jnp.concatenate([zero32, sm, pad], axis=-1)
    sb = jnp.concatenate([-sm, zero32, pad], axis=-1)
    cg, sg = angles(GQA_HEAD_DIM)
    cos_g = jnp.concatenate([cg, cg], axis=-1)
    sin_g = jnp.concatenate([-sg, sg], axis=-1)
    return cos_m, sa, sb, cos_g, sin_g


def kernel(x_prompt, x_sample, cache_mla_ckv, cache_mla_krope, cache_gqa_k, cache_gqa_v, state_s5, c, c_ctx, w_ada, b_ada, norm_ffn1, ffn1_w_gate, ffn1_w_up, ffn1_w_down, norm_mix, w_in, mla_q_norm, mla_w_uq, mla_kv_norm, mla_w_ukv, s5_lambda_re, s5_lambda_im, s5_log_dt, s5_b_re, s5_b_im, s5_c_re, s5_c_im, s5_d, s5_w_glu, s5_b_glu, gqa_q_norm, gqa_k_norm, w_out, norm_ffn2, ffn2_w_gate, ffn2_w_up, ffn2_w_down, norm_final):
    L = DEPTH
    cb, cs, _ = x_prompt.shape
    lb, ls, _ = x_sample.shape
    past = cache_mla_ckv.shape[2]

    zpad = jnp.zeros((L, D_MODEL, LANE - MLA_ROPE), F32)
    w_in_p = jnp.concatenate([w_in[:, :, :MLA_Q_RANK + MLA_KV_RANK + MLA_ROPE], zpad,
                              w_in[:, :, MLA_Q_RANK + MLA_KV_RANK + MLA_ROPE:]], axis=-1).astype(BF16)
    wuq = mla_w_uq.reshape(L, MLA_Q_RANK, MLA_HEADS, MLA_NOPE + MLA_ROPE)
    wuq = jnp.pad(wuq, ((0, 0), (0, 0), (0, 0), (0, MLA_QK_PAD - MLA_NOPE - MLA_ROPE)))
    wuq = wuq.reshape(L, MLA_Q_RANK, MLA_HEADS * MLA_QK_PAD).astype(BF16)
    wukv = mla_w_ukv.reshape(L, MLA_KV_RANK, MLA_HEADS, MLA_NOPE + MLA_V)
    wukv = jnp.concatenate([wukv[..., :MLA_NOPE].reshape(L, MLA_KV_RANK, MLA_HEADS * MLA_NOPE),
                            wukv[..., MLA_NOPE:].reshape(L, MLA_KV_RANK, MLA_HEADS * MLA_V)], axis=-1).astype(BF16)
    w = {
        "norm_mix": norm_mix.reshape(L, 1, D_MODEL),
        "w_in": w_in_p,
        "mla_q_norm": mla_q_norm.reshape(L, 1, MLA_Q_RANK),
        "mla_w_uq": wuq,
        "mla_kv_norm": mla_kv_norm.reshape(L, 1, MLA_KV_RANK),
        "mla_w_ukv": wukv,
        "gqa_q_norm": gqa_q_norm.reshape(L, 1, GQA_HEAD_DIM),
        "gqa_k_norm": gqa_k_norm.reshape(L, 1, GQA_HEAD_DIM),
        "s5_w_glu": s5_w_glu.astype(BF16),
        "s5_b_glu": s5_b_glu.reshape(L, 1, S5_CH),
        "w_out": w_out.astype(BF16),
    }
    f1 = (norm_ffn1.reshape(L, 1, D_MODEL), ffn1_w_gate.astype(BF16), ffn1_w_up.astype(BF16), ffn1_w_down.astype(BF16))
    f2 = (norm_ffn2.reshape(L, 1, D_MODEL), ffn2_w_gate.astype(BF16), ffn2_w_up.astype(BF16), ffn2_w_down.astype(BF16))
    bw, cw, ar, ai = _s5_weights(s5_lambda_re, s5_lambda_im, s5_log_dt, s5_b_re, s5_b_im, s5_c_re, s5_c_im)
    d_skip = s5_d.reshape(L, 1, S5_CH)
    rope_ctx = _rope_tables(256, True)
    rope_lat = _rope_tables(ls, False)
    cache_kr_pad = jnp.pad(cache_mla_krope, ((0, 0), (0, 0), (0, 0), (0, LANE - MLA_ROPE)))
    cache_k = cache_gqa_k.reshape(lb, L, past, GQA_KV_HEADS * GQA_HEAD_DIM)
    cache_vt = jnp.swapaxes(cache_gqa_v.reshape(lb, L, past, GQA_KV_HEADS * GQA_HEAD_DIM), 2, 3).astype(BF16)
    h0 = state_s5.reshape(lb, L, 2, S5_NSTATE, 2)
    zero_state = jnp.zeros((2, cb, S5_NSTATE), F32)

    cmat = jnp.concatenate([c_ctx[None, :], c, jnp.zeros((MOD_ROWS - 1 - lb, D_MODEL), F32)], axis=0)
    mods = _modulation(cmat, w_ada, b_ada)

    xp = x_prompt.reshape(cb * cs, D_MODEL)
    xs = x_sample.reshape(lb * ls, D_MODEL)
    st_ckv, st_kr, st_k, st_v, st_s5 = [], [], [], [], []

    def mixer(x, bsz, seq, mods_l, mod_base, per_batch_mod, l, rope, is_ctx):
        outs = _inproj(x, bsz, seq, mods_l, mod_base, per_batch_mod, w, l, rope, is_ctx)
        qm, km, vm, u_tm, gq, gk, gv = outs[:7]
        grep = GQA_Q_HEADS // GQA_KV_HEADS
        mla_hps, gqa_hps = (MLA_HEADS, GQA_Q_HEADS) if is_ctx else (ATTN_LAT_HPS, ATTN_LAT_HPS)
        mla_parts = [_kv_part_2d(km, vm, seq, MLA_QK_PAD, MLA_V, MLA_HEADS, mla_hps)]
        gqa_parts = [_kv_part_2d(gk, gv, seq, GQA_HEAD_DIM, GQA_HEAD_DIM, GQA_KV_HEADS, gqa_hps // grep)]
        if is_ctx:
            h0r = h0i = zero_state
            tq, tt = 256, 32
        else:
            km_c, vm_c = _kvup(cache_mla_ckv, cache_kr_pad, w["mla_w_ukv"], l)
            mla_parts = [_kv_part_2d(km_c, vm_c, past, MLA_QK_PAD, MLA_V, MLA_HEADS, mla_hps)] + mla_parts
            gqa_parts = [_kv_part_cache(cache_k, cache_vt, past, GQA_HEAD_DIM, GQA_HEAD_DIM, gqa_hps // grep, l)] + gqa_parts
            h0r = jnp.swapaxes(h0[:, l, :, :, 0], 0, 1)
            h0i = jnp.swapaxes(h0[:, l, :, :, 1], 0, 1)
            tq, tt = 1024, 64
        o_mla = _attention(qm, mla_parts, bsz, seq, MLA_HEADS, 1, MLA_QK_PAD, MLA_V, tq, mla_hps)
        o_gqa = _attention(gq, gqa_parts, bsz, seq, GQA_Q_HEADS, grep, GQA_HEAD_DIM, GQA_HEAD_DIM, tq, gqa_hps)
        y_f, y_b, fin_r, fin_i = _s5_scan(u_tm, d_skip, bw, cw, ar, ai, h0r, h0i, bsz, seq, tt, l)
        x = _outproj(x, bsz, seq, mods_l, mod_base, per_batch_mod, o_mla, y_f, y_b, o_gqa, w, l)
        extra = None
        if is_ctx:
            ckv, kr, kg, vg = outs[7:]
            s5s = jnp.swapaxes(jnp.stack([fin_r, fin_i], axis=-1), 0, 1)
            extra = (ckv.reshape(bsz, seq, MLA_KV_RANK), kr.reshape(bsz, seq, MLA_ROPE),
                     kg.reshape(bsz, seq, GQA_KV_HEADS, GQA_HEAD_DIM), vg.reshape(bsz, seq, GQA_KV_HEADS, GQA_HEAD_DIM),
                     s5s.reshape(bsz, 2, S5_GROUPS, S5_STATE, 2))
        return x, extra

    g_final = norm_final.reshape(1, D_MODEL)
    for l in range(L):
        mods_l = mods[l]
        fin = g_final if l == L - 1 else None
        xp = _ffn(xp, mods_l, 0, cb * cs, *f1, l, 0)
        xp, (ckv, kr, kg, vg, s5s) = mixer(xp, cb, cs, mods_l, 0, False, l, rope_ctx, True)
        xp = _ffn(xp, mods_l, 0, cb * cs, *f2, l, 6, fin)
        st_ckv.append(ckv)
        st_kr.append(kr)
        st_k.append(kg)
        st_v.append(vg)
        st_s5.append(s5s)
        xs = _ffn(xs, mods_l, 1, ls, *f1, l, 0)
        xs, _ = mixer(xs, lb, ls, mods_l, 1, True, l, rope_lat, False)
        xs = _ffn(xs, mods_l, 1, ls, *f2, l, 6, fin)

    y_prompt = xp.reshape(cb, cs, D_MODEL)
    y_sample = xs.reshape(lb, ls, D_MODEL)
    return (y_prompt, y_sample, jnp.stack(st_ckv, axis=1), jnp.stack(st_kr, axis=1), jnp.stack(st_k, axis=1),
            jnp.stack(st_v, axis=1), jnp.stack(st_s5, axis=1))
```

```python
import functools
import math

import jax
import jax.numpy as jnp
from jax import lax
from jax.experimental import pallas as pl
from jax.experimental.pallas import tpu as pltpu

D_MODEL = 2048
DEPTH = 4
GRID_W = 64
ROPE_THETA = 10000.0
EPS = 1e-6
N_MOD = 9
D_FF = 2 * D_MODEL

MLA_HEADS = 8
MLA_Q_RANK = D_MODEL // 4
MLA_KV_RANK = D_MODEL // 8
MLA_NOPE = 128
MLA_ROPE = 64
MLA_V = 128
MLA_QK_PAD = 256

S5_CH = D_MODEL // 4
S5_GROUP = 16
S5_GROUPS = S5_CH // S5_GROUP
S5_STATE = 64
S5_NSTATE = S5_GROUPS * S5_STATE

GQA_Q_HEADS = 4
GQA_KV_HEADS = 2
GQA_HEAD_DIM = 128

MLA_SCALE = 1.0 / math.sqrt(MLA_NOPE + MLA_ROPE)
GQA_SCALE = 1.0 / math.sqrt(GQA_HEAD_DIM)
LOG2E = math.log2(math.e)
MLA_QSCALE = MLA_SCALE * LOG2E
GQA_QSCALE = GQA_SCALE * LOG2E

ZQ0, ZKV0, ZKR0, ZU0, ZGQ0, ZGK0, ZGV0, Z_END = 0, 512, 768, 896, 1408, 1920, 2176, 2432

LANE = 128
SUBLANE = 8
MOD_ROWS = 16
MOD_TN = 1024

F32 = jnp.float32
BF16 = jnp.bfloat16

VMEM_LIMIT = 56 * 1024 * 1024
INPROJ_TM = 256
OUTPROJ_TM = 512
RESIDENT = pl.Buffered(1)


def _cparams(sem):
    return pltpu.CompilerParams(dimension_semantics=sem, vmem_limit_bytes=VMEM_LIMIT)


def _rms(x, g):
    return x * lax.rsqrt(jnp.mean(x * x, axis=-1, keepdims=True) + EPS) * g


def _sigmoid(x):
    return 1.0 / (1.0 + jnp.exp(-x))


def _dot(a, b):
    return jnp.dot(a, b, preferred_element_type=F32)


def _mod_kernel(c_ref, w_ref, b_ref, o_ref):
    c = c_ref[...]
    a = (c * _sigmoid(c)).astype(BF16)
    o_ref[...] = _dot(a, w_ref[...].astype(BF16)) + b_ref[...]


def _modulation(cmat, w_ada, b_ada):
    tn = MOD_TN
    nd = N_MOD * D_MODEL
    out = pl.pallas_call(
        _mod_kernel,
        grid=(DEPTH, nd // tn),
        in_specs=[
            pl.BlockSpec((MOD_ROWS, D_MODEL), lambda l, j: (0, 0)),
            pl.BlockSpec((None, D_MODEL, tn), lambda l, j: (l, 0, j)),
            pl.BlockSpec((None, 1, tn), lambda l, j: (l, 0, j)),
        ],
        out_specs=pl.BlockSpec((None, MOD_ROWS, tn), lambda l, j: (l, 0, j)),
        out_shape=jax.ShapeDtypeStruct((DEPTH, MOD_ROWS, nd), F32),
        compiler_params=_cparams(("parallel", "parallel")),
        name="modulation",
    )(cmat, w_ada, b_ada.reshape(DEPTH, 1, nd))
    return out.reshape(DEPTH, MOD_ROWS, N_MOD, D_MODEL)


def _ffn_kernel(x_ref, mod_ref, g_ref, wg_hbm, wu_hbm, wd_hbm, *rest, k0, final, l):
    if final:
        gf_ref, o_ref, wg_buf, wu_buf, wd_buf, hm_ref, sem = rest
    else:
        o_ref, wg_buf, wu_buf, wd_buf, hm_ref, sem = rest
    i = pl.program_id(0)
    nff = D_FF // FFN_TF

    def tile_copies(j, slot):
        cols = pl.ds(j * FFN_TF, FFN_TF)
        return (pltpu.make_async_copy(wg_hbm.at[l, :, cols], wg_buf.at[slot], sem.at[0, slot]),
                pltpu.make_async_copy(wu_hbm.at[l, :, cols], wu_buf.at[slot], sem.at[1, slot]),
                pltpu.make_async_copy(wd_hbm.at[l, cols, :], wd_buf.at[slot], sem.at[2, slot]))

    @pl.when(i == 0)
    def _():
        for c in tile_copies(0, 0):
            c.start()

    hm_ref[...] = (_rms(x_ref[...], g_ref[...]) * (1.0 + mod_ref[k0 + 1:k0 + 2, :])
                   + mod_ref[k0:k0 + 1, :]).astype(BF16)
    for j in range(nff):
        slot = j % 2
        if j + 1 < nff:
            for c in tile_copies(j + 1, 1 - slot):
                c.start()
        else:
            @pl.when(i + 1 < pl.num_programs(0))
            def _():
                for c in tile_copies(0, 1 - slot):
                    c.start()
        for c in tile_copies(j, slot):
            c.wait()
        for c0 in range(0, FFN_TF, FFN_SUB):
            h = hm_ref[...]
            g = _dot(h, wg_buf[slot, :, c0:c0 + FFN_SUB])
            u = _dot(h, wu_buf[slot, :, c0:c0 + FFN_SUB])
            a = (g * _sigmoid(g)) * u
            d = _dot(a.astype(BF16), wd_buf[slot, c0:c0 + FFN_SUB, :])
            if j == 0 and c0 == 0:
                o_ref[...] = d
            else:
                o_ref[...] += d
    out = x_ref[...] + (0.5 * mod_ref[k0 + 2:k0 + 3, :]) * o_ref[...]
    if final:
        out = _rms(out, gf_ref[...])
    o_ref[...] = out


FFN_TM = 512
FFN_TF = 1024
FFN_SUB = 512
assert (D_FF // FFN_TF) % 2 == 0


def _ffn(x, mods_l, mod_base, rows_per_mod, norm_g, wg, wu, wd, l, k0, final_g=None):
    n = x.shape[0]
    tm = FFN_TM
    in_specs = [
        pl.BlockSpec((tm, D_MODEL), lambda i: (i, 0)),
        pl.BlockSpec((None, N_MOD, D_MODEL), lambda i: (mod_base + (i * tm) // rows_per_mod, 0, 0)),
        pl.BlockSpec((None, 1, D_MODEL), lambda i: (l, 0, 0)),
        pl.BlockSpec(memory_space=pl.ANY),
        pl.BlockSpec(memory_space=pl.ANY),
        pl.BlockSpec(memory_space=pl.ANY),
    ]
    args = [x, mods_l, norm_g, wg, wu, wd]
    if final_g is not None:
        in_specs.append(pl.BlockSpec((1, D_MODEL), lambda i: (0, 0)))
        args.append(final_g)
    return pl.pallas_call(
        functools.partial(_ffn_kernel, k0=k0, final=final_g is not None, l=l),
        grid=(n // tm,),
        in_specs=in_specs,
        out_specs=pl.BlockSpec((tm, D_MODEL), lambda i: (i, 0)),
        out_shape=jax.ShapeDtypeStruct((n, D_MODEL), F32),
        scratch_shapes=[
            pltpu.VMEM((2, D_MODEL, FFN_TF), BF16),
            pltpu.VMEM((2, D_MODEL, FFN_TF), BF16),
            pltpu.VMEM((2, FFN_TF, D_MODEL), BF16),
            pltpu.VMEM((tm, D_MODEL), BF16),
            pltpu.SemaphoreType.DMA((3, 2)),
        ],
        compiler_params=_cparams(("arbitrary",)),
        name="ffn",
    )(*args)


def _rope_mla(t, cos, sa, sb):
    return t * cos + pltpu.roll(t, 32, 1) * sa + pltpu.roll(t, 96, 1) * sb


def _rope_gqa(t, cos, ss):
    return t * cos + pltpu.roll(t, 64, 1) * ss


def _inproj_kernel(x_ref, mod_ref, g_ref, win_ref, qn_ref, wuq_ref, kvn_ref, wukv_ref, gqn_ref, gkn_ref,
                   cm_ref, sa_ref, sb_ref, cg_ref, sg_ref,
                   qm_ref, km_ref, vm_ref, u_ref, gq_ref, gk_ref, gv_ref, *cache_refs):
    x = x_ref[...]
    h = (_rms(x, g_ref[...]) * (1.0 + mod_ref[4:5, :]) + mod_ref[3:4, :]).astype(BF16)
    z = _dot(h, win_ref[...])
    cm, sa, sb = cm_ref[...], sa_ref[...], sb_ref[...]
    cg, sg = cg_ref[...], sg_ref[...]

    ckv = _rms(z[:, ZKV0:ZKR0], kvn_ref[...])
    kr = _rope_mla(z[:, ZKR0:ZU0], cm, sa, sb)
    kr_b = kr.astype(BF16)

    q = _dot(_rms(z[:, ZQ0:ZKV0], qn_ref[...]).astype(BF16), wuq_ref[...])
    kv = _dot(ckv.astype(BF16), wukv_ref[...])
    for hd in range(MLA_HEADS):
        c0 = hd * MLA_QK_PAD
        qm_ref[:, c0:c0 + LANE] = (q[:, c0:c0 + LANE] * MLA_QSCALE).astype(BF16)
        qm_ref[:, c0 + LANE:c0 + 2 * LANE] = (
            _rope_mla(q[:, c0 + LANE:c0 + 2 * LANE], cm, sa, sb) * MLA_QSCALE).astype(BF16)
        km_ref[:, c0:c0 + LANE] = kv[:, hd * LANE:(hd + 1) * LANE].astype(BF16)
        km_ref[:, c0 + LANE:c0 + 2 * LANE] = kr_b
    vm_ref[...] = kv[:, MLA_HEADS * MLA_NOPE:].T.astype(BF16)

    u_ref[...] = z[:, ZU0:ZGQ0]

    for hd in range(GQA_Q_HEADS):
        t = _rms(z[:, ZGQ0 + hd * LANE:ZGQ0 + (hd + 1) * LANE], gqn_ref[...])
        gq_ref[:, hd * LANE:(hd + 1) * LANE] = (_rope_gqa(t, cg, sg) * GQA_QSCALE).astype(BF16)
    kgs = []
    for hd in range(GQA_KV_HEADS):
        t = _rms(z[:, ZGK0 + hd * LANE:ZGK0 + (hd + 1) * LANE], gkn_ref[...])
        t = _rope_gqa(t, cg, sg)
        kgs.append(t)
        gk_ref[:, hd * LANE:(hd + 1) * LANE] = t.astype(BF16)
    gv = z[:, ZGV0:Z_END]
    gv_ref[...] = gv.T.astype(BF16)

    if cache_refs:
        ckv_o, kr_o, kg_o, vg_o = cache_refs
        ckv_o[...] = ckv
        kr_o[...] = kr[:, :MLA_ROPE]
        for hd in range(GQA_KV_HEADS):
            kg_o[:, hd * LANE:(hd + 1) * LANE] = kgs[hd]
        vg_o[...] = gv


def _inproj(x, bsz, seq, mods_l, mod_base, per_batch_mod, w, l, rope, emit_cache):
    n = bsz * seq
    tm = min(INPROJ_TM, seq)
    nt = seq // tm
    cm, sa, sb, cg, sg = rope
    rope_blocks = cm.shape[0] // tm
    rmap = (lambda b, j: (j, 0)) if rope_blocks > 1 else (lambda b, j: (0, 0))
    row = lambda b, j: (b * nt + j, 0)
    lsel = lambda b, j: (l, 0, 0)
    mod_map = (lambda b, j: (mod_base + b, 0, 0)) if per_batch_mod else (lambda b, j: (mod_base, 0, 0))
    out_shape = [
        jax.ShapeDtypeStruct((n, MLA_HEADS * MLA_QK_PAD), BF16),
        jax.ShapeDtypeStruct((n, MLA_HEADS * MLA_QK_PAD), BF16),
        jax.ShapeDtypeStruct((bsz * MLA_HEADS * MLA_V, seq), BF16),
        jax.ShapeDtypeStruct((seq, bsz * S5_CH), F32),
        jax.ShapeDtypeStruct((n, GQA_Q_HEADS * GQA_HEAD_DIM), BF16),
        jax.ShapeDtypeStruct((n, GQA_KV_HEADS * GQA_HEAD_DIM), BF16),
        jax.ShapeDtypeStruct((bsz * GQA_KV_HEADS * GQA_HEAD_DIM, seq), BF16),
    ]
    out_specs = [
        pl.BlockSpec((tm, MLA_HEADS * MLA_QK_PAD), row),
        pl.BlockSpec((tm, MLA_HEADS * MLA_QK_PAD), row),
        pl.BlockSpec((MLA_HEADS * MLA_V, tm), lambda b, j: (b, j)),
        pl.BlockSpec((tm, S5_CH), lambda b, j: (j, b)),
        pl.BlockSpec((tm, GQA_Q_HEADS * GQA_HEAD_DIM), row),
        pl.BlockSpec((tm, GQA_KV_HEADS * GQA_HEAD_DIM), row),
        pl.BlockSpec((GQA_KV_HEADS * GQA_HEAD_DIM, tm), lambda b, j: (b, j)),
    ]
    if emit_cache:
        out_shape += [
            jax.ShapeDtypeStruct((n, MLA_KV_RANK), F32),
            jax.ShapeDtypeStruct((n, MLA_ROPE), F32),
            jax.ShapeDtypeStruct((n, GQA_KV_HEADS * GQA_HEAD_DIM), F32),
            jax.ShapeDtypeStruct((n, GQA_KV_HEADS * GQA_HEAD_DIM), F32),
        ]
        out_specs += [
            pl.BlockSpec((tm, MLA_KV_RANK), row),
            pl.BlockSpec((tm, MLA_ROPE), row),
            pl.BlockSpec((tm, GQA_KV_HEADS * GQA_HEAD_DIM), row),
            pl.BlockSpec((tm, GQA_KV_HEADS * GQA_HEAD_DIM), row),
        ]
    return pl.pallas_call(
        _inproj_kernel,
        grid=(bsz, nt),
        in_specs=[
            pl.BlockSpec((tm, D_MODEL), row),
            pl.BlockSpec((None, N_MOD, D_MODEL), mod_map),
            pl.BlockSpec((None, 1, D_MODEL), lsel),
            pl.BlockSpec((None, D_MODEL, Z_END), lsel, pipeline_mode=RESIDENT),
            pl.BlockSpec((None, 1, MLA_Q_RANK), lsel),
            pl.BlockSpec((None, MLA_Q_RANK, MLA_HEADS * MLA_QK_PAD), lsel, pipeline_mode=RESIDENT),
            pl.BlockSpec((None, 1, MLA_KV_RANK), lsel),
            pl.BlockSpec((None, MLA_KV_RANK, MLA_HEADS * (MLA_NOPE + MLA_V)), lsel, pipeline_mode=RESIDENT),
            pl.BlockSpec((None, 1, GQA_HEAD_DIM), lsel),
            pl.BlockSpec((None, 1, GQA_HEAD_DIM), lsel),
        ] + [pl.BlockSpec((tm, LANE), rmap)] * 5,
        out_specs=out_specs,
        out_shape=out_shape,
        compiler_params=_cparams(("parallel", "parallel")),
        name="inproj",
    )(x, mods_l, w["norm_mix"], w["w_in"], w["mla_q_norm"], w["mla_w_uq"], w["mla_kv_norm"], w["mla_w_ukv"],
      w["gqa_q_norm"], w["gqa_k_norm"], cm, sa, sb, cg, sg)


def _kvup_kernel(ckv_ref, kr_ref, wukv_ref, km_ref, vm_ref):
    kv = _dot(ckv_ref[...].astype(BF16), wukv_ref[...])
    kr_b = kr_ref[...].astype(BF16)
    for hd in range(MLA_HEADS):
        c0 = hd * MLA_QK_PAD
        km_ref[:, c0:c0 + LANE] = kv[:, hd * LANE:(hd + 1) * LANE].astype(BF16)
        km_ref[:, c0 + LANE:c0 + 2 * LANE] = kr_b
    vm_ref[...] = kv[:, MLA_HEADS * MLA_NOPE:].T.astype(BF16)


def _kvup(cache_ckv, cache_kr_pad, wukv, l):
    bsz, _, past, _ = cache_ckv.shape
    n = bsz * past
    return pl.pallas_call(
        _kvup_kernel,
        grid=(bsz,),
        in_specs=[
            pl.BlockSpec((None, None, past, MLA_KV_RANK), lambda b: (b, l, 0, 0)),
            pl.BlockSpec((None, None, past, LANE), lambda b: (b, l, 0, 0)),
            pl.BlockSpec((None, MLA_KV_RANK, MLA_HEADS * (MLA_NOPE + MLA_V)), lambda b: (l, 0, 0)),
        ],
        out_specs=[
            pl.BlockSpec((past, MLA_HEADS * MLA_QK_PAD), lambda b: (b, 0)),
            pl.BlockSpec((MLA_HEADS * MLA_V, past), lambda b: (b, 0)),
        ],
        out_shape=[
            jax.ShapeDtypeStruct((n, MLA_HEADS * MLA_QK_PAD), BF16),
            jax.ShapeDtypeStruct((bsz * MLA_HEADS * MLA_V, past), BF16),
        ],
        compiler_params=_cparams(("parallel",)),
        name="kvup",
    )(cache_ckv, cache_kr_pad, wukv)


ATTN_CK = 256
ATTN_QK_ROWS = 2048
ATTN_TQ = 512
ATTN_SUB = 512
ATTN_LAT_HPS = 4


def _attn_kernel(*refs, nparts, hps, rep, dk, dv):
    q_ref = refs[0]
    k_refs = refs[1:1 + nparts]
    vt_refs = refs[1 + nparts:1 + 2 * nparts]
    o_ref = refs[1 + 2 * nparts]
    s_ref = refs[2 + 2 * nparts]
    tq = q_ref.shape[0]
    ts = min(tq, ATTN_SUB)
    dn = (((1,), (1,)), ((), ()))
    def key_chunks(size):
        out, off = [], 0
        for pi, k_ref in enumerate(k_refs):
            rows = k_ref.shape[0]
            for r0 in range(0, rows, size):
                ck = min(size, rows - r0)
                out.append((pi, r0, ck, off))
                off += ck
        return out

    chunks = key_chunks(ATTN_CK)
    items = [(hh, q0) for hh in range(hps) for q0 in range(0, tq, ts)]

    maxes = []
    for hh, q0 in items:
        kv = hh // rep
        q = q_ref[q0:q0 + ts, hh * dk:(hh + 1) * dk]
        mpart = None
        for pi, r0, ck, off in key_chunks(ATTN_QK_ROWS):
            k = k_refs[pi][r0:r0 + ck, kv * dk:(kv + 1) * dk].astype(BF16)
            s = lax.dot_general(k, q, dn, preferred_element_type=F32)
            s_ref[hh, off:off + ck, q0:q0 + ts] = s
            blk = jnp.max(s.reshape(ck // SUBLANE, SUBLANE, ts), axis=0)
            mpart = blk if mpart is None else jnp.maximum(mpart, blk)
        maxes.append(jnp.max(mpart, axis=0, keepdims=True))

    for (hh, q0), m in zip(items, maxes):
        kv = hh // rep
        lpart = jnp.zeros((SUBLANE, ts), F32)
        acc = jnp.zeros((dv, ts), F32)
        for pi, r0, ck, off in chunks:
            p = jnp.exp2(s_ref[hh, off:off + ck, q0:q0 + ts] - m)
            lpart = lpart + jnp.sum(p.reshape(ck // SUBLANE, SUBLANE, ts), axis=0)
            vt = vt_refs[pi][kv * dv:(kv + 1) * dv, r0:r0 + ck].astype(BF16)
            acc = acc + _dot(vt, p.astype(BF16))
        den = jnp.sum(lpart, axis=0, keepdims=True)
        o_ref[q0:q0 + ts, hh * dv:(hh + 1) * dv] = (acc / den).T.astype(o_ref.dtype)


def _attention(q, parts, bsz, seq, heads, rep, dk, dv, tq, hps):
    nq = seq // tq
    n_keys = sum(p[4] for p in parts)
    assert all(p[4] % LANE == 0 for p in parts) and heads % hps == 0 and hps % rep == 0
    in_specs = [pl.BlockSpec((tq, hps * dk), lambda b, h, i: (b * nq + i, h))]
    in_specs += [p[2] for p in parts] + [p[3] for p in parts]
    args = [q] + [p[0] for p in parts] + [p[1] for p in parts]
    return pl.pallas_call(
        functools.partial(_attn_kernel, nparts=len(parts), hps=hps, rep=rep, dk=dk, dv=dv),
        grid=(bsz, heads // hps, nq),
        in_specs=in_specs,
        out_specs=pl.BlockSpec((tq, hps * dv), lambda b, h, i: (b * nq + i, h)),
        out_shape=jax.ShapeDtypeStruct((bsz * seq, heads * dv), BF16),
        scratch_shapes=[pltpu.VMEM((hps, n_keys, tq), F32)],
        compiler_params=_cparams(("parallel", "parallel", "arbitrary")),
        name="attention",
    )(*args)


def _kv_part_2d(k, vt, rows, dk, dv, kv_heads, kps):
    ng = kv_heads // kps
    return (k, vt,
            pl.BlockSpec((rows, kps * dk), lambda b, h, i: (b, h)),
            pl.BlockSpec((kps * dv, rows), lambda b, h, i: (b * ng + h, 0)), rows)


def _kv_part_cache(k, vt, rows, dk, dv, kps, l):
    return (k, vt,
            pl.BlockSpec((None, None, rows, kps * dk), lambda b, h, i: (b, l, 0, h)),
            pl.BlockSpec((None, None, kps * dv, rows), lambda b, h, i: (b, l, h, 0)), rows)


S5_LC = 512
S5_Q = S5_CH // LANE
S5_QS = S5_NSTATE // S5_Q
S5_TILE_GROUPS = LANE // S5_GROUP
S5_TILE_ROWS = 512


def _s5_kernel(uf_ref, ub_ref, d_ref, bw_ref, cw_ref, ar_ref, ai_ref, h0r_ref, h0i_ref,
               yf_ref, yb_ref, fr_ref, fi_ref, ut_ref, yt_ref, hr_ref, hi_ref, sr_ref, si_ref, *, bsz, tt):
    j = pl.program_id(0)

    @pl.when(j == 0)
    def _():
        sr_ref[...] = h0r_ref[...]
        si_ref[...] = h0i_ref[...]

    for d, (u_ref, y_ref) in enumerate(((uf_ref, yf_ref), (ub_ref, yb_ref))):
        for b in range(bsz):
            for q in range(S5_Q):
                c0 = b * S5_CH + q * LANE
                ut_ref[d, q, pl.ds(b, tt, stride=bsz), :] = u_ref[:, c0:c0 + LANE]
        for q in range(S5_Q):
            bu = _dot(ut_ref[d, q].astype(BF16), bw_ref[d, q])
            hr_ref[d, :, q * S5_QS:(q + 1) * S5_QS] = bu[:, :S5_QS]
            hi_ref[d, :, q * S5_QS:(q + 1) * S5_QS] = bu[:, S5_QS:]

        for c in range(S5_NSTATE // S5_LC):
            lanes = slice(c * S5_LC, (c + 1) * S5_LC)
            ar = jnp.broadcast_to(ar_ref[d, :, lanes], (bsz, S5_LC))
            ai = jnp.broadcast_to(ai_ref[d, :, lanes], (bsz, S5_LC))
            pr = sr_ref[d, :, lanes]
            pi = si_ref[d, :, lanes]
            for t in (range(tt - 1, -1, -1) if d == 1 else range(tt)):
                rows = slice(t * bsz, (t + 1) * bsz)
                nr = ar * pr - ai * pi + hr_ref[d, rows, lanes]
                ni = ar * pi + ai * pr + hi_ref[d, rows, lanes]
                hr_ref[d, rows, lanes] = nr
                hi_ref[d, rows, lanes] = ni
                pr, pi = nr, ni
            sr_ref[d, :, lanes] = pr
            si_ref[d, :, lanes] = pi

        for q in range(S5_Q):
            hs = slice(q * S5_QS, (q + 1) * S5_QS)
            cols = slice(q * LANE, (q + 1) * LANE)
            yq = (_dot(hr_ref[d, :, hs].astype(BF16), cw_ref[d, q, :S5_QS, :])
                  + _dot(hi_ref[d, :, hs].astype(BF16), cw_ref[d, q, S5_QS:, :]))
            if d == 0:
                yq = d_ref[:, cols] * ut_ref[d, q] + yq
            yt_ref[d, q] = yq
        for b in range(bsz):
            for q in range(S5_Q):
                c0 = b * S5_CH + q * LANE
                y_ref[:, c0:c0 + LANE] = yt_ref[d, q, pl.ds(b, tt, stride=bsz), :]

    fr_ref[...] = sr_ref[...]
    fi_ref[...] = si_ref[...]


def _s5_scan(u, d_skip, bw, cw, ar, ai, h0r, h0i, bsz, seq, tt, l):
    rows = tt * bsz
    nt = seq // tt
    fmap = lambda j: (j, 0)
    bmap = lambda j: (nt - 1 - j, 0)
    wsel = lambda j: (l, 0, 0, 0, 0)
    asel = lambda j: (l, 0, 0, 0)
    ssel = lambda j: (0, 0, 0)
    return pl.pallas_call(
        functools.partial(_s5_kernel, bsz=bsz, tt=tt),
        grid=(nt,),
        in_specs=[
            pl.BlockSpec((tt, bsz * S5_CH), fmap),
            pl.BlockSpec((tt, bsz * S5_CH), bmap),
            pl.BlockSpec((None, 1, S5_CH), lambda j: (l, 0, 0)),
            pl.BlockSpec((None, 2, S5_Q, LANE, 2 * S5_QS), wsel),
            pl.BlockSpec((None, 2, S5_Q, 2 * S5_QS, LANE), wsel),
            pl.BlockSpec((None, 2, 1, S5_NSTATE), asel),
            pl.BlockSpec((None, 2, 1, S5_NSTATE), asel),
            pl.BlockSpec((2, bsz, S5_NSTATE), ssel),
            pl.BlockSpec((2, bsz, S5_NSTATE), ssel),
        ],
        out_specs=[
            pl.BlockSpec((tt, bsz * S5_CH), fmap),
            pl.BlockSpec((tt, bsz * S5_CH), bmap),
            pl.BlockSpec((2, bsz, S5_NSTATE), ssel),
            pl.BlockSpec((2, bsz, S5_NSTATE), ssel),
        ],
        out_shape=[
            jax.ShapeDtypeStruct((seq, bsz * S5_CH), F32),
            jax.ShapeDtypeStruct((seq, bsz * S5_CH), F32),
            jax.ShapeDtypeStruct((2, bsz, S5_NSTATE), F32),
            jax.ShapeDtypeStruct((2, bsz, S5_NSTATE), F32),
        ],
        scratch_shapes=[
            pltpu.VMEM((2, S5_Q, rows, LANE), F32),
            pltpu.VMEM((2, S5_Q, rows, LANE), F32),
            pltpu.VMEM((2, rows, S5_NSTATE), F32),
            pltpu.VMEM((2, rows, S5_NSTATE), F32),
            pltpu.VMEM((2, bsz, S5_NSTATE), F32),
            pltpu.VMEM((2, bsz, S5_NSTATE), F32),
        ],
        compiler_params=_cparams(("arbitrary",)),
        name="s5_scan",
    )(u, u, d_skip, bw, cw, ar, ai, h0r, h0i)


def _s5_weights(lam_re, lam_im, log_dt, b_re, b_im, c_re, c_im):
    dt = jnp.exp(log_dt)[..., None]
    mag = jnp.exp(lam_re * dt)
    ab_re = mag * jnp.cos(lam_im * dt)
    ab_im = mag * jnp.sin(lam_im * dt)
    den = lam_re * lam_re + lam_im * lam_im
    nr = ab_re - 1.0
    f_re = (nr * lam_re + ab_im * lam_im) / den
    f_im = (ab_im * lam_re - nr * lam_im) / den
    bb_re = f_re[..., None] * b_re - f_im[..., None] * b_im
    bb_im = f_re[..., None] * b_im + f_im[..., None] * b_re
    eye = jnp.eye(S5_TILE_GROUPS, dtype=F32)
    lshape = lam_re.shape[:2]

    def pack_b(bb):
        t = bb.reshape(*lshape, S5_Q, S5_TILE_GROUPS, S5_STATE, S5_GROUP)
        return jnp.einsum("ldqrpc,rs->ldqrcsp", t, eye).reshape(*lshape, S5_Q, LANE, S5_QS)

    def pack_c(cc):
        t = cc.reshape(*lshape, S5_Q, S5_TILE_GROUPS, S5_GROUP, S5_STATE)
        return jnp.einsum("ldqrcp,rs->ldqrpsc", t, eye).reshape(*lshape, S5_Q, S5_QS, LANE)

    bw = jnp.concatenate([pack_b(bb_re), pack_b(bb_im)], axis=-1).astype(BF16)
    cw = jnp.concatenate([pack_c(c_re), -pack_c(c_im)], axis=-2).astype(BF16)
    ar = ab_re.reshape(*lshape, 1, S5_NSTATE)
    ai = ab_im.reshape(*lshape, 1, S5_NSTATE)
    return bw, cw, ar, ai


def _outproj_kernel(x_ref, mod_ref, om_ref, yf_ref, yb_ref, og_ref, wglu_ref, bglu_ref, wout_ref, o_ref):
    y = yf_ref[...] + yb_ref[...]
    gate = _sigmoid(_dot(y.astype(BF16), wglu_ref[...]) + bglu_ref[...])
    gelu = 0.5 * y * (1.0 + jnp.tanh(math.sqrt(2.0 / math.pi) * (y + 0.044715 * (y * y * y))))
    s5 = (gelu * gate).astype(BF16)
    n_mla = MLA_HEADS * MLA_V
    mixed = (_dot(om_ref[...], wout_ref[0:n_mla, :])
             + _dot(s5, wout_ref[n_mla:n_mla + S5_CH, :])
             + _dot(og_ref[...], wout_ref[n_mla + S5_CH:, :]))
    o_ref[...] = x_ref[...] + mod_ref[5:6, :] * mixed


def _outproj(x, bsz, seq, mods_l, mod_base, per_batch_mod, o_mla, y_f, y_b, o_gqa, w, l):
    n = bsz * seq
    tm = min(OUTPROJ_TM, seq)
    nt = seq // tm
    row = lambda b, j: (b * nt + j, 0)
    lsel = lambda b, j: (l, 0, 0)
    mod_map = (lambda b, j: (mod_base + b, 0, 0)) if per_batch_mod else (lambda b, j: (mod_base, 0, 0))
    return pl.pallas_call(
        _outproj_kernel,
        grid=(bsz, nt),
        in_specs=[
            pl.BlockSpec((tm, D_MODEL), row),
            pl.BlockSpec((None, N_MOD, D_MODEL), mod_map),
            pl.BlockSpec((tm, MLA_HEADS * MLA_V), row),
            pl.BlockSpec((tm, S5_CH), lambda b, j: (j, b)),
            pl.BlockSpec((tm, S5_CH), lambda b, j: (j, b)),
            pl.BlockSpec((tm, GQA_Q_HEADS * GQA_HEAD_DIM), row),
            pl.BlockSpec((None, S5_CH, S5_CH), lsel),
            pl.BlockSpec((None, 1, S5_CH), lsel),
            pl.BlockSpec((None, D_MODEL, D_MODEL), lsel, pipeline_mode=RESIDENT),
        ],
        out_specs=pl.BlockSpec((tm, D_MODEL), row),
        out_shape=jax.ShapeDtypeStruct((n, D_MODEL), F32),
        compiler_params=_cparams(("parallel", "parallel")),
        name="outproj",
    )(x, mods_l, o_mla, y_f, y_b, o_gqa, w["s5_w_glu"], w["s5_b_glu"], w["w_out"])


def _rope_tables(seq, identity):
    if identity:
        one = jnp.ones((seq, LANE), F32)
        zero = jnp.zeros((seq, LANE), F32)
        return one, zero, zero, one, zero
    n_rows = seq // GRID_W
    row = jnp.broadcast_to(jnp.arange(n_rows, dtype=F32)[:, None], (n_rows, GRID_W)).reshape(seq)
    col = jnp.broadcast_to(jnp.arange(GRID_W, dtype=F32)[None, :], (n_rows, GRID_W)).reshape(seq)

    def angles(rot_dim):
        n_freq = rot_dim // 4
        inv = ROPE_THETA ** (-jnp.arange(n_freq, dtype=F32) / n_freq)
        ang = jnp.concatenate([row[:, None] * inv, col[:, None] * inv], axis=-1)
        return jnp.cos(ang), jnp.sin(ang)

    cm, sm = angles(MLA_ROPE)
    zero32 = jnp.zeros_like(sm)
    pad = jnp.zeros((seq, LANE - MLA_ROPE), F32)
    cos_m = jnp.concatenate([cm, cm, jnp.ones_like(pad)], axis=-1)
    sa = jnp.concatenate([zero32, sm, pad], axis=-1)
    sb = jnp.concatenate([-sm, zero32, pad], axis=-1)
    cg, sg = angles(GQA_HEAD_DIM)
    cos_g = jnp.concatenate([cg, cg], axis=-1)
    sin_g = jnp.concatenate([-sg, sg], axis=-1)
    return cos_m, sa, sb, cos_g, sin_g


def kernel(x_prompt, x_sample, cache_mla_ckv, cache_mla_krope, cache_gqa_k, cache_gqa_v, state_s5, c, c_ctx, w_ada, b_ada, norm_ffn1, ffn1_w_gate, ffn1_w_up, ffn1_w_down, norm_mix, w_in, mla_q_norm, mla_w_uq, mla_kv_norm, mla_w_ukv, s5_lambda_re, s5_lambda_im, s5_log_dt, s5_b_re, s5_b_im, s5_c_re, s5_c_im, s5_d, s5_w_glu, s5_b_glu, gqa_q_norm, gqa_k_norm, w_out, norm_ffn2, ffn2_w_gate, ffn2_w_up, ffn2_w_down, norm_final):
    L = DEPTH
    cb, cs, _ = x_prompt.shape
    lb, ls, _ = x_sample.shape
    past = cache_mla_ckv.shape[2]

    zpad = jnp.zeros((L, D_MODEL, LANE - MLA_ROPE), F32)
    w_in_p = jnp.concatenate([w_in[:, :, :MLA_Q_RANK + MLA_KV_RANK + MLA_ROPE], zpad,
                              w_in[:, :, MLA_Q_RANK + MLA_KV_RANK + MLA_ROPE:]], axis=-1).astype(BF16)
    wuq = mla_w_uq.reshape(L, MLA_Q_RANK, MLA_HEADS, MLA_NOPE + MLA_ROPE)
    wuq = jnp.pad(wuq, ((0, 0), (0, 0), (0, 0), (0, MLA_QK_PAD - MLA_NOPE - MLA_ROPE)))
    wuq = wuq.reshape(L, MLA_Q_RANK, MLA_HEADS * MLA_QK_PAD).astype(BF16)
    wukv = mla_w_ukv.reshape(L, MLA_KV_RANK, MLA_HEADS, MLA_NOPE + MLA_V)
    wukv = jnp.concatenate([wukv[..., :MLA_NOPE].reshape(L, MLA_KV_RANK, MLA_HEADS * MLA_NOPE),
                            wukv[..., MLA_NOPE:].reshape(L, MLA_KV_RANK, MLA_HEADS * MLA_V)], axis=-1).astype(BF16)
    w = {
        "norm_mix": norm_mix.reshape(L, 1, D_MODEL),
        "w_in": w_in_p,
        "mla_q_norm": mla_q_norm.reshape(L, 1, MLA_Q_RANK),
        "mla_w_uq": wuq,
        "mla_kv_norm": mla_kv_norm.reshape(L, 1, MLA_KV_RANK),
        "mla_w_ukv": wukv,
        "gqa_q_norm": gqa_q_norm.reshape(L, 1, GQA_HEAD_DIM),
        "gqa_k_norm": gqa_k_norm.reshape(L, 1, GQA_HEAD_DIM),
        "s5_w_glu": s5_w_glu.astype(BF16),
        "s5_b_glu": s5_b_glu.reshape(L, 1, S5_CH),
        "w_out": w_out.astype(BF16),
    }
    f1 = (norm_ffn1.reshape(L, 1, D_MODEL), ffn1_w_gate.astype(BF16), ffn1_w_up.astype(BF16), ffn1_w_down.astype(BF16))
    f2 = (norm_ffn2.reshape(L, 1, D_MODEL), ffn2_w_gate.astype(BF16), ffn2_w_up.astype(BF16), ffn2_w_down.astype(BF16))
    bw, cw, ar, ai = _s5_weights(s5_lambda_re, s5_lambda_im, s5_log_dt, s5_b_re, s5_b_im, s5_c_re, s5_c_im)
    d_skip = s5_d.reshape(L, 1, S5_CH)
    rope_ctx = _rope_tables(min(INPROJ_TM, cs), True)
    rope_lat = _rope_tables(ls, False)
    cache_kr_pad = jnp.pad(cache_mla_krope, ((0, 0), (0, 0), (0, 0), (0, LANE - MLA_ROPE)))
    cache_k = cache_gqa_k.reshape(lb, L, past, GQA_KV_HEADS * GQA_HEAD_DIM)
    cache_vt = jnp.swapaxes(cache_gqa_v.reshape(lb, L, past, GQA_KV_HEADS * GQA_HEAD_DIM), 2, 3).astype(BF16)
    h0 = state_s5.reshape(lb, L, 2, S5_NSTATE, 2)
    zero_state = jnp.zeros((2, cb, S5_NSTATE), F32)

    cmat = jnp.concatenate([c_ctx[None, :], c, jnp.zeros((MOD_ROWS - 1 - lb, D_MODEL), F32)], axis=0)
    mods = _modulation(cmat, w_ada, b_ada)

    xp = x_prompt.reshape(cb * cs, D_MODEL)
    xs = x_sample.reshape(lb * ls, D_MODEL)
    st_ckv, st_kr, st_k, st_v, st_s5 = [], [], [], [], []

    def mixer(x, bsz, seq, mods_l, mod_base, per_batch_mod, l, rope, is_ctx):
        outs = _inproj(x, bsz, seq, mods_l, mod_base, per_batch_mod, w, l, rope, is_ctx)
        qm, km, vm, u_tm, gq, gk, gv = outs[:7]
        grep = GQA_Q_HEADS // GQA_KV_HEADS
        mla_hps, gqa_hps = (MLA_HEADS, GQA_Q_HEADS) if is_ctx else (ATTN_LAT_HPS, ATTN_LAT_HPS)
        mla_parts = [_kv_part_2d(km, vm, seq, MLA_QK_PAD, MLA_V, MLA_HEADS, mla_hps)]
        gqa_parts = [_kv_part_2d(gk, gv, seq, GQA_HEAD_DIM, GQA_HEAD_DIM, GQA_KV_HEADS, gqa_hps // grep)]
        tq = min(ATTN_TQ, seq)
        tt = S5_TILE_ROWS // bsz
        if is_ctx:
            h0r = h0i = zero_state
        else:
            km_c, vm_c = _kvup(cache_mla_ckv, cache_kr_pad, w["mla_w_ukv"], l)
            mla_parts = [_kv_part_2d(km_c, vm_c, past, MLA_QK_PAD, MLA_V, MLA_HEADS, mla_hps)] + mla_parts
            gqa_parts = [_kv_part_cache(cache_k, cache_vt, past, GQA_HEAD_DIM, GQA_HEAD_DIM, gqa_hps // grep, l)] + gqa_parts
            h0r = jnp.swapaxes(h0[:, l, :, :, 0], 0, 1)
            h0i = jnp.swapaxes(h0[:, l, :, :, 1], 0, 1)
        o_mla = _attention(qm, mla_parts, bsz, seq, MLA_HEADS, 1, MLA_QK_PAD, MLA_V, tq, mla_hps)
        o_gqa = _attention(gq, gqa_parts, bsz, seq, GQA_Q_HEADS, grep, GQA_HEAD_DIM, GQA_HEAD_DIM, tq, gqa_hps)
        y_f, y_b, fin_r, fin_i = _s5_scan(u_tm, d_skip, bw, cw, ar, ai, h0r, h0i, bsz, seq, tt, l)
        x = _outproj(x, bsz, seq, mods_l, mod_base, per_batch_mod, o_mla, y_f, y_b, o_gqa, w, l)
        extra = None
        if is_ctx:
            ckv, kr, kg, vg = outs[7:]
            s5s = jnp.swapaxes(jnp.stack([fin_r, fin_i], axis=-1), 0, 1)
            extra = (ckv.reshape(bsz, seq, MLA_KV_RANK), kr.reshape(bsz, seq, MLA_ROPE),
                     kg.reshape(bsz, seq, GQA_KV_HEADS, GQA_HEAD_DIM), vg.reshape(bsz, seq, GQA_KV_HEADS, GQA_HEAD_DIM),
                     s5s.reshape(bsz, 2, S5_GROUPS, S5_STATE, 2))
        return x, extra

    g_final = norm_final.reshape(1, D_MODEL)
    for l in range(L):
        mods_l = mods[l]
        fin = g_final if l == L - 1 else None
        xp = _ffn(xp, mods_l, 0, cb * cs, *f1, l, 0)
        xp, (ckv, kr, kg, vg, s5s) = mixer(xp, cb, cs, mods_l, 0, False, l, rope_ctx, True)
        xp = _ffn(xp, mods_l, 0, cb * cs, *f2, l, 6, fin)
        st_ckv.append(ckv)
        st_kr.append(kr)
        st_k.append(kg)
        st_v.append(vg)
        st_s5.append(s5s)
        xs = _ffn(xs, mods_l, 1, ls, *f1, l, 0)
        xs, _ = mixer(xs, lb, ls, mods_l, 1, True, l, rope_lat, False)
        xs = _ffn(xs, mods_l, 1, ls, *f2, l, 6, fin)

    y_prompt = xp.reshape(cb, cs, D_MODEL)
    y_sample = xs.reshape(lb, ls, D_MODEL)
    return (y_prompt, y_sample, jnp.stack(st_ckv, axis=1), jnp.stack(st_kr, axis=1), jnp.stack(st_k, axis=1),
            jnp.stack(st_v, axis=1), jnp.stack(st_s5, axis=1))
```
